```python
import jax, jax.numpy as jnp
from jax import lax
import numpy as np

D_MODEL = 1024
BATCH = 16
SEQ = 2048
DEPTH = 1

CHUNK = 64
N_META = 16
PAD = CHUNK - N_META
A_KDIM = 128
A_HEADS = D_MODEL // A_KDIM
A_FDIM = A_HEADS * A_KDIM
A_WIDTH = D_MODEL // 2
A_VDIM = A_WIDTH // A_HEADS
POOL_WINDOWS = (2, 4, 8, 16)
B_GROUPS = 4
B_WIDTH = D_MODEL // 2
B_GDIM = B_WIDTH // B_GROUPS
N_BRANCH = 2
SPLITS = (A_FDIM, 2 * A_FDIM, 2 * A_FDIM + A_WIDTH, 2 * A_FDIM + 2 * A_WIDTH,
          2 * A_FDIM + 2 * A_WIDTH + B_WIDTH, 2 * A_FDIM + 2 * A_WIDTH + B_WIDTH + D_MODEL)
IN_COLS = 2 * A_FDIM + 2 * A_WIDTH + B_WIDTH + N_BRANCH * D_MODEL
D_FF = ((8 * D_MODEL // 3 + 127) // 128) * 128
CONV_W = 3
EPS = 1e-6

kernel_name = "hybrid_hgrn2_pool_convffn_block"


def rmsnorm(x, g):
    xf = x.astype(jnp.float32)
    y = xf * lax.rsqrt(jnp.mean(xf * xf, axis=-1, keepdims=True) + EPS)
    return (y * g.astype(jnp.float32)).astype(x.dtype)


def hgrn2_mixer(q, f_logit, i, og, lb, head_g):
    dt = i.dtype
    f32 = jnp.float32
    B_, T, _ = q.shape
    n_chunks = (T + PAD) // CHUNK
    lbf = lb.astype(f32)
    f = lbf + (1.0 - lbf) * jax.nn.sigmoid(f_logit.astype(f32))
    k = 1.0 - f
    logf = jnp.log(f)
    qf = jax.nn.silu(q.astype(f32))
    vf = i.astype(f32)

    def to_chunks(t, d):
        t = jnp.pad(t, ((0, 0), (PAD, 0), (0, 0)))
        return t.reshape(B_, n_chunks, CHUNK, A_HEADS, d).transpose(1, 0, 3, 2, 4)

    qc = to_chunks(qf, A_KDIM)
    kc = to_chunks(k, A_KDIM)
    vc = to_chunks(vf, A_VDIM)
    ac = jnp.cumsum(to_chunks(logf, A_KDIM), axis=3)
    causal = jnp.tril(jnp.ones((CHUNK, CHUNK), dtype=bool))[:, :, None]

    def step(S, inp):
        q_, k_, v_, a_ = inp
        o_inter = jnp.einsum('bhtk,bhkv->bhtv', q_ * jnp.exp(a_), S)
        diff = a_[:, :, :, None, :] - a_[:, :, None, :, :]
        decay = jnp.exp(jnp.where(causal, diff, -jnp.inf))
        scores = jnp.einsum('bhtsk,bhsk->bhts', q_[:, :, :, None, :] * decay, k_)
        o = o_inter + jnp.einsum('bhts,bhsv->bhtv', scores, v_)
        a_last = a_[:, :, -1:, :]
        S = jnp.exp(a_last[:, :, 0, :])[..., None] * S + jnp.einsum(
            'bhsk,bhsv->bhkv', k_ * jnp.exp(a_last - a_), v_)
        return S, o

    S0 = jnp.zeros((B_, A_HEADS, A_KDIM, A_VDIM), f32)
    _, o = lax.scan(step, S0, (qc, kc, vc, ac))
    o = o.transpose(1, 0, 3, 2, 4).reshape(B_, n_chunks * CHUNK, A_HEADS, A_VDIM)[:, PAD:]
    o = o * lax.rsqrt(jnp.mean(o * o, axis=-1, keepdims=True) + EPS)
    o = o * head_g.astype(f32).reshape(A_HEADS, A_VDIM)
    o = o.reshape(B_, T, A_WIDTH) * jax.nn.silu(og.astype(f32))
    return o.astype(dt)


def pool_mixer(p, w_group, scale):
    dt = p.dtype
    B_, T, _ = p.shape
    pf = p.astype(jnp.float32).reshape(B_, T, B_GROUPS, B_GDIM)
    cs = jnp.cumsum(pf, axis=1)
    pos = jnp.arange(1, T + 1, dtype=jnp.float32)
    outs = []
    for gi, w in enumerate(POOL_WINDOWS):
        c = cs[:, :, gi]
        lag = jnp.pad(c[:, :-w], ((0, 0), (w, 0), (0, 0)))
        mean = (c - lag) / jnp.minimum(pos, float(w))[None, :, None]
        outs.append(mean - pf[:, :, gi])
    d = jnp.stack(outs, axis=2)
    y = jnp.einsum('btgc,gcd->btgd', d, w_group.astype(jnp.float32)).reshape(B_, T, B_WIDTH)
    return (y * scale.astype(jnp.float32)).astype(dt)


def causal_dwconv(u, w, b):
    C = u.shape[-1]
    up = jnp.pad(u, ((0, 0), (CONV_W - 1, 0), (0, 0)))
    y = lax.conv_general_dilated(up, w[:, None, :].astype(u.dtype), window_strides=(1,), padding='VALID',
                                 dimension_numbers=('NWC', 'WIO', 'NWC'), feature_group_count=C)
    return y + b


def setup_inputs(seed: int = 0) -> dict:
    key = jax.random.key(seed)
    ks = jax.random.split(key, 20)
    f32 = jnp.float32
    nrm = lambda k, shape, s: jax.random.normal(k, shape, f32) * s
    L = DEPTH
    return {
        "x": nrm(ks[0], (BATCH, SEQ, D_MODEL), 1.0),
        "meta_tokens": nrm(ks[1], (N_META, D_MODEL), 1.0),
        "lb_logits": nrm(ks[2], (DEPTH + 1, A_FDIM), 1.0),
        "norm1_g": 1.0 + nrm(ks[3], (L, D_MODEL), 0.05),
        "w_in": nrm(ks[4], (L, D_MODEL, IN_COLS), D_MODEL ** -0.5),
        "b_f": nrm(ks[5], (L, A_FDIM), 0.1),
        "head_norm_g": 1.0 + nrm(ks[6], (L, A_WIDTH), 0.05),
        "w_pool": nrm(ks[7], (L, B_GROUPS, B_GDIM, B_GDIM), B_GDIM ** -0.5),
        "pool_scale": 1.0 + nrm(ks[8], (L, B_WIDTH), 0.1),
        "w_branch_a": nrm(ks[9], (L, A_WIDTH, D_MODEL), A_WIDTH ** -0.5),
        "w_branch_b": nrm(ks[10], (L, B_WIDTH, D_MODEL), B_WIDTH ** -0.5),
        "w_out": nrm(ks[11], (L, D_MODEL, D_MODEL), D_MODEL ** -0.5),
        "norm2_g": 1.0 + nrm(ks[12], (L, D_MODEL), 0.05),
        "w_up": nrm(ks[13], (L, D_MODEL, 2 * D_FF), D_MODEL ** -0.5),
        "conv_w": nrm(ks[14], (L, CONV_W, D_FF), CONV_W ** -0.5),
        "conv_b": nrm(ks[15], (L, D_FF), 0.01),
        "w_down": nrm(ks[16], (L, D_FF, D_MODEL), D_FF ** -0.5),
        "final_norm_g": 1.0 + nrm(ks[17], (D_MODEL,), 0.05),
    }


def reference(x, meta_tokens, lb_logits, norm1_g, w_in, b_f, head_norm_g, w_pool, pool_scale,
              w_branch_a, w_branch_b, w_out, norm2_g, w_up, conv_w, conv_b, w_down, final_norm_g):
    B_ = x.shape[0]
    meta = jnp.broadcast_to(meta_tokens.astype(x.dtype)[None], (B_, N_META, D_MODEL))
    h = jnp.concatenate([meta, x], axis=1)
    lb_all = jnp.cumsum(jax.nn.softmax(lb_logits.astype(jnp.float32), axis=0), axis=0)
    for l in range(DEPTH):
        z = rmsnorm(h, norm1_g[l])
        proj = z @ w_in[l]
        q, fl, iv, og, pin, ga, gb = jnp.split(proj, SPLITS, axis=-1)
        o_a = hgrn2_mixer(q, fl + b_f[l], iv, og, lb_all[l], head_norm_g[l])
        o_b = pool_mixer(pin, w_pool[l], pool_scale[l])
        mix = jax.nn.sigmoid(ga) * (o_a @ w_branch_a[l]) + jax.nn.sigmoid(gb) * (o_b @ w_branch_b[l])
        h = h + mix @ w_out[l]
        z = rmsnorm(h, norm2_g[l])
        u, v = jnp.split(z @ w_up[l], 2, axis=-1)
        u = causal_dwconv(u, conv_w[l], conv_b[l])
        h = h + (jax.nn.silu(u) * v) @ w_down[l]
    h = rmsnorm(h, final_norm_g)
    return h[:, N_META:]
```

```python
import functools

import jax
import jax.numpy as jnp
from jax import lax
from jax.experimental import pallas as pl
from jax.experimental.pallas import tpu as pltpu

D_MODEL = 1024
CHUNK = 64
N_META = 16
PAD = CHUNK - N_META
A_KDIM = 128
A_HEADS = D_MODEL // A_KDIM
A_FDIM = A_HEADS * A_KDIM
A_WIDTH = D_MODEL // 2
A_VDIM = A_WIDTH // A_HEADS
POOL_WINDOWS = (2, 4, 8, 16)
B_GROUPS = 4
B_WIDTH = D_MODEL // 2
B_GDIM = B_WIDTH // B_GROUPS
D_FF = ((8 * D_MODEL // 3 + 127) // 128) * 128
CONV_W = 3
EPS = 1e-6
IN_COLS = 2 * A_FDIM + 2 * A_WIDTH + B_WIDTH + 2 * D_MODEL

SUB = 8
N_SUB = CHUNK // SUB
POOL_HALO = 16
CONV_HALO = 8
TILE_CHUNKS = 11
TM = TILE_CHUNKS * CHUNK
PROJ_BLK = 512
FF_BLK = 256
VMEM_LIMIT_BYTES = 60000 * 1024

F32 = jnp.float32
BF16 = jnp.bfloat16


def _rmsnorm(x, g):
    ms = jnp.mean(x * x, axis=-1, keepdims=True)
    return x * lax.rsqrt(ms + EPS) * g


def _split3(x):
    h1 = x.astype(BF16)
    r1 = x - h1.astype(F32)
    h2 = r1.astype(BF16)
    h3 = (r1 - h2.astype(F32)).astype(BF16)
    return h1, h2, h3


def _mixer_kernel(h_ref, lbl_ref, g1_ref, win_ref, bf_ref, hg_ref, wpool_ref, pscale_ref,
                  wa_ref, wb_ref, wout_ref, o_ref,
                  z_s, q_s, lf_s, lk_s, v_s, og_s, pin_s, ga_s, gb_s, oa_s, ob_s, mix_s,
                  st_s, a_s, b_s):
    t_idx = pl.program_id(1)

    @pl.when(t_idx == 0)
    def _():
        st_s[...] = jnp.zeros_like(st_s)
        pin_s[0:POOL_HALO, :] = jnp.zeros((POOL_HALO, B_WIDTH), F32)

    z_s[...] = _rmsnorm(h_ref[0], g1_ref[...]).astype(BF16)

    l0 = lbl_ref[0:1, :]
    l1 = lbl_ref[1:2, :]
    lmax = jnp.maximum(l0, l1)
    e0 = jnp.exp(l0 - lmax)
    e1 = jnp.exp(l1 - lmax)
    lb = e0 / (e0 + e1)

    def proj(col0):
        return jnp.dot(z_s[...], win_ref[:, col0:col0 + PROJ_BLK], preferred_element_type=F32)

    for j in range(A_FDIM // PROJ_BLK):
        c = j * PROJ_BLK
        q_s[:, c:c + PROJ_BLK] = jax.nn.silu(proj(c))
    for j in range(A_FDIM // PROJ_BLK):
        c = j * PROJ_BLK
        x = proj(A_FDIM + c) + bf_ref[:, c:c + PROJ_BLK]
        sig = jax.nn.sigmoid(x)
        lbj = lb[:, c:c + PROJ_BLK]
        lf_s[:, c:c + PROJ_BLK] = jnp.log(lbj + (1.0 - lbj) * sig)
        lk_s[:, c:c + PROJ_BLK] = jnp.log((1.0 - lbj) * (1.0 - sig))
    base = 2 * A_FDIM
    v_s[...] = proj(base)
    og_s[...] = jax.nn.silu(proj(base + A_WIDTH))
    pin_s[POOL_HALO:, :] = proj(base + 2 * A_WIDTH)
    base = base + 2 * A_WIDTH + B_WIDTH
    for j in range(D_MODEL // PROJ_BLK):
        c = j * PROJ_BLK
        ga_s[:, c:c + PROJ_BLK] = jax.nn.sigmoid(proj(base + c)).astype(BF16)
        gb_s[:, c:c + PROJ_BLK] = jax.nn.sigmoid(proj(base + D_MODEL + c)).astype(BF16)

    row_i = lax.broadcasted_iota(jnp.int32, (CHUNK, CHUNK), 0)
    col_i = lax.broadcasted_iota(jnp.int32, (CHUNK, CHUNK), 1)
    tril = (col_i <= row_i).astype(BF16)
    lane_i = lax.broadcasted_iota(jnp.int32, (SUB, CHUNK), 1)
    subl_i = lax.broadcasted_iota(jnp.int32, (SUB, CHUNK), 0)

    def chunk_body(c, carry):
        r0 = pl.multiple_of(c * CHUNK, CHUNK)
        rows = pl.ds(r0, CHUNK)
        h1, h2, h3 = _split3(lf_s[rows, :])
        a_all = (jnp.dot(tril, h1, preferred_element_type=F32)
                 + jnp.dot(tril, h2, preferred_element_type=F32)
                 + jnp.dot(tril, h3, preferred_element_type=F32))
        a_s[...] = a_all
        b_s[...] = a_all - lk_s[rows, :]

        for h in range(A_HEADS):
            kl = slice(h * A_KDIM, (h + 1) * A_KDIM)
            vl = slice(h * A_VDIM, (h + 1) * A_VDIM)
            q = q_s[rows, kl]
            a = a_s[:, kl]
            b = b_s[:, kl]
            v = v_s[rows, vl].astype(BF16)
            st = st_s[h]
            a_last = a_s[CHUNK - 1:CHUNK, kl]

            qa = (q * jnp.exp(a)).astype(BF16)
            o = pl.dot(qa, st.astype(BF16), trans_b=True)

            blocks = []
            for i in range(N_SUB):
                r = slice(i * SUB, (i + 1) * SUB)
                q_i = q[r]
                a_i = a[r]
                acc = jnp.zeros((SUB, CHUNK), F32)
                for s in range(SUB):
                    b_row = b_s[i * SUB + s:i * SUB + s + 1, kl]
                    col = jnp.sum(q_i * jnp.exp(a_i - b_row), axis=-1, keepdims=True)
                    acc = jnp.where(lane_i == i * SUB + s, col, acc)
                blk = jnp.where(lane_i <= i * SUB + subl_i, acc, 0.0)
                if i > 0:
                    anchor = a_s[i * SUB - 1:i * SUB, kl]
                    qs = (q_i * jnp.exp(a_i - anchor)).astype(BF16)
                    kp = jnp.exp(anchor - b[0:i * SUB])
                    kp = jnp.concatenate([kp, jnp.zeros((CHUNK - i * SUB, A_KDIM), F32)], axis=0)
                    blk = blk + pl.dot(qs, kp.astype(BF16), trans_b=True)
                blocks.append(blk)
            scores = jnp.concatenate(blocks, axis=0).astype(BF16)
            o = o + jnp.dot(scores, v, preferred_element_type=F32)

            ms = jnp.mean(o * o, axis=-1, keepdims=True)
            o = o * lax.rsqrt(ms + EPS) * hg_ref[:, vl] * og_s[rows, vl]
            oa_s[rows, vl] = o.astype(BF16)

            kd = jnp.exp(a_last - b).astype(BF16)
            st_s[h] = st * jnp.exp(a_last) + pl.dot(v, kd, trans_a=True)
        return carry

    lax.fori_loop(0, TILE_CHUNKS, chunk_body, 0)

    row_e = lax.broadcasted_iota(jnp.int32, (TM, B_GDIM), 0)
    pos = (t_idx * TM + row_e - (PAD - 1)).astype(F32)
    for gi, w in enumerate(POOL_WINDOWS):
        gl = slice(gi * B_GDIM, (gi + 1) * B_GDIM)
        xs = pin_s[:, gl]
        acc = xs
        span = 1
        while span < w:
            acc = acc + pltpu.roll(acc, span, 0)
            span *= 2
        cnt = jnp.clip(pos, 1.0, float(w))
        d = acc[POOL_HALO:] / cnt - xs[POOL_HALO:]
        y = jnp.dot(d.astype(BF16), wpool_ref[gi], preferred_element_type=F32)
        ob_s[:, gl] = (y * pscale_ref[:, gl]).astype(BF16)
    pin_s[0:POOL_HALO, :] = pin_s[TM:TM + POOL_HALO, :]

    for j in range(D_MODEL // PROJ_BLK):
        cs = slice(j * PROJ_BLK, (j + 1) * PROJ_BLK)
        ya = jnp.dot(oa_s[...], wa_ref[:, cs], preferred_element_type=F32)
        yb = jnp.dot(ob_s[...], wb_ref[:, cs], preferred_element_type=F32)
        mix_s[:, cs] = (ga_s[:, cs].astype(F32) * ya + gb_s[:, cs].astype(F32) * yb).astype(BF16)
    for j in range(D_MODEL // PROJ_BLK):
        cs = slice(j * PROJ_BLK, (j + 1) * PROJ_BLK)
        o_ref[0, :, cs] = h_ref[0, :, cs] + jnp.dot(mix_s[...], wout_ref[:, cs], preferred_element_type=F32)


def _ffn_kernel(h_ref, g2_ref, wup_ref, cw_ref, cb_ref, wdn_ref, gf_ref, o_ref,
                z_s, u_s, halo_s, acc_s):
    t_idx = pl.program_id(1)

    @pl.when(t_idx == 0)
    def _():
        halo_s[...] = jnp.zeros_like(halo_s)

    z_s[...] = _rmsnorm(h_ref[0], g2_ref[...]).astype(BF16)
    for j in range(D_FF // FF_BLK):
        cs = slice(j * FF_BLK, (j + 1) * FF_BLK)
        u = jnp.dot(z_s[...], wup_ref[:, cs], preferred_element_type=F32)
        v = jnp.dot(z_s[...], wup_ref[:, D_FF + j * FF_BLK:D_FF + (j + 1) * FF_BLK],
                    preferred_element_type=F32)
        u_s[0:CONV_HALO, :] = halo_s[j]
        u_s[CONV_HALO:, :] = u
        halo_s[j] = u_s[TM:TM + CONV_HALO, :]
        uc = cb_ref[:, cs] + cw_ref[2:3, cs] * u
        for k in range(CONV_W - 1):
            lag = CONV_W - 1 - k
            uc = uc + cw_ref[k:k + 1, cs] * u_s[CONV_HALO - lag:CONV_HALO - lag + TM, :]
        g = (jax.nn.silu(uc) * v).astype(BF16)
        part = jnp.dot(g, wdn_ref[cs, :], preferred_element_type=F32)
        if j == 0:
            acc_s[...] = part
        else:
            acc_s[...] += part
    h2 = h_ref[0] + acc_s[...]
    o_ref[0] = _rmsnorm(h2, gf_ref[...])


def _const_spec(shape):
    nd = len(shape)
    return pl.BlockSpec(shape, lambda b, t: (0,) * nd, pipeline_mode=pl.Buffered(1))


def kernel(x, meta_tokens, lb_logits, norm1_g, w_in, b_f, head_norm_g, w_pool, pool_scale, w_branch_a,
           w_branch_b, w_out, norm2_g, w_up, conv_w, conv_b, w_down, final_norm_g):
    B, S, D = x.shape
    T_pad = PAD + N_META + S
    assert D == D_MODEL and T_pad % TM == 0
    n_t = T_pad // TM

    meta = jnp.broadcast_to(meta_tokens.astype(x.dtype)[None], (B, N_META, D))
    h = jnp.concatenate([jnp.zeros((B, PAD, D), x.dtype), meta, x], axis=1)

    row = lambda p: p.reshape(1, -1).astype(F32)
    tile_spec = pl.BlockSpec((1, TM, D), lambda b, t: (b, t, 0))
    params = pltpu.CompilerParams(dimension_semantics=("arbitrary", "arbitrary"),
                                  vmem_limit_bytes=VMEM_LIMIT_BYTES)

    mixer_in = [
        h, lb_logits.astype(F32), row(norm1_g[0]), w_in[0].astype(BF16), row(b_f[0]), row(head_norm_g[0]),
        w_pool[0].astype(BF16), row(pool_scale[0]), w_branch_a[0].astype(BF16), w_branch_b[0].astype(BF16),
        w_out[0].astype(BF16),
    ]
    h1 = pl.pallas_call(
        _mixer_kernel,
        grid=(B, n_t),
        in_specs=[tile_spec] + [_const_spec(a.shape) for a in mixer_in[1:]],
        out_specs=tile_spec,
        out_shape=jax.ShapeDtypeStruct((B, T_pad, D), F32),
        scratch_shapes=[
            pltpu.VMEM((TM, D), BF16),
            pltpu.VMEM((TM, A_FDIM), F32),
            pltpu.VMEM((TM, A_FDIM), F32),
            pltpu.VMEM((TM, A_FDIM), F32),
            pltpu.VMEM((TM, A_WIDTH), F32),
            pltpu.VMEM((TM, A_WIDTH), F32),
            pltpu.VMEM((POOL_HALO + TM, B_WIDTH), F32),
            pltpu.VMEM((TM, D), BF16),
            pltpu.VMEM((TM, D), BF16),
            pltpu.VMEM((TM, A_WIDTH), BF16),
            pltpu.VMEM((TM, B_WIDTH), BF16),
            pltpu.VMEM((TM, D), BF16),
            pltpu.VMEM((A_HEADS, A_VDIM, A_KDIM), F32),
            pltpu.VMEM((CHUNK, A_FDIM), F32),
            pltpu.VMEM((CHUNK, A_FDIM), F32),
        ],
        compiler_params=params,
        name="mixer",
    )(*mixer_in)

    ffn_in = [
        h1, row(norm2_g[0]), w_up[0].astype(BF16), conv_w[0].astype(F32), row(conv_b[0]),
        w_down[0].astype(BF16), row(final_norm_g),
    ]
    out = pl.pallas_call(
        _ffn_kernel,
        grid=(B, n_t),
        in_specs=[tile_spec] + [_const_spec(a.shape) for a in ffn_in[1:]],
        out_specs=tile_spec,
        out_shape=jax.ShapeDtypeStruct((B, T_pad, D), F32),
        scratch_shapes=[
            pltpu.VMEM((TM, D), BF16),
            pltpu.VMEM((CONV_HALO + TM, FF_BLK), F32),
            pltpu.VMEM((D_FF // FF_BLK, CONV_HALO, FF_BLK), F32),
            pltpu.VMEM((TM, D), F32),
        ],
        compiler_params=params,
        name="ffn",
    )(*ffn_in)
    return out[:, PAD + N_META:]
```

```python
import functools

import jax
import jax.numpy as jnp
from jax import lax
from jax.experimental import pallas as pl
from jax.experimental.pallas import tpu as pltpu

D_MODEL = 1024
CHUNK = 64
N_META = 16
PAD = CHUNK - N_META
A_KDIM = 128
A_HEADS = D_MODEL // A_KDIM
A_FDIM = A_HEADS * A_KDIM
A_WIDTH = D_MODEL // 2
A_VDIM = A_WIDTH // A_HEADS
POOL_WINDOWS = (2, 4, 8, 16)
B_GROUPS = 4
B_WIDTH = D_MODEL // 2
B_GDIM = B_WIDTH // B_GROUPS
D_FF = ((8 * D_MODEL // 3 + 127) // 128) * 128
CONV_W = 3
EPS = 1e-6

SUB = 8
N_SUB = CHUNK // SUB
BF16_ROWS = 16
POOL_HALO = 16
CONV_HALO = 8
MAIN_TM = 8 * CHUNK
PROJ_BLK = 512
FF_BLK = 256
N_FF_BLK = D_FF // FF_BLK
VMEM_LIMIT_BYTES = 60000 * 1024

F32 = jnp.float32
BF16 = jnp.bfloat16


def _rmsnorm(x, g):
    ms = jnp.mean(x * x, axis=-1, keepdims=True)
    return x * lax.rsqrt(ms + EPS) * g


def _mixer_kernel(tm, pos0, emit_carry,
                  h_ref, st0_ref, pin0_ref, lbl_ref, g1_ref, win_ref, bf_ref, hg_ref, wpool_ref, pscale_ref,
                  wa_ref, wb_ref, wout_ref, *rest):
    if emit_carry:
        o_ref, st_out_ref, pin_out_ref = rest[:3]
        rest = rest[3:]
    else:
        o_ref = rest[0]
        rest = rest[1:]
    (z_s, q_s, a_s, b_s, qa_s, qs_s, kd_s, v_s, og_s, pin_s, ga_s, gb_s, oa_s, ob_s, st_s) = rest
    n_chunks = tm // CHUNK
    t_idx = pl.program_id(1)

    @pl.when(t_idx == 0)
    def _():
        st_s[...] = st0_ref[...]
        pin_s[0:POOL_HALO, :] = pin0_ref[...]

    z_s[...] = _rmsnorm(h_ref[0], g1_ref[...]).astype(BF16)

    l0 = lbl_ref[0:1, :]
    l1 = lbl_ref[1:2, :]
    lmax = jnp.maximum(l0, l1)
    e0 = jnp.exp(l0 - lmax)
    e1 = jnp.exp(l1 - lmax)
    lb = e0 / (e0 + e1)

    def proj(col0):
        return jnp.dot(z_s[...], win_ref[:, col0:col0 + PROJ_BLK], preferred_element_type=F32)

    for j in range(A_FDIM // PROJ_BLK):
        c = j * PROJ_BLK
        q_s[:, c:c + PROJ_BLK] = jax.nn.silu(proj(c))
    for j in range(A_FDIM // PROJ_BLK):
        c = j * PROJ_BLK
        x = proj(A_FDIM + c) + bf_ref[:, c:c + PROJ_BLK]
        sig = jax.nn.sigmoid(x)
        lbj = lb[:, c:c + PROJ_BLK]
        a_s[:, c:c + PROJ_BLK] = jnp.log2(lbj + (1.0 - lbj) * sig)
        b_s[:, c:c + PROJ_BLK] = jnp.log2((1.0 - lbj) * (1.0 - sig))
    base = 2 * A_FDIM
    v_s[...] = proj(base).astype(BF16)
    og_s[...] = jax.nn.silu(proj(base + A_WIDTH)).astype(BF16)
    pin_s[POOL_HALO:, :] = proj(base + 2 * A_WIDTH)
    base = base + 2 * A_WIDTH + B_WIDTH
    for j in range(D_MODEL // PROJ_BLK):
        c = j * PROJ_BLK
        ga_s[:, c:c + PROJ_BLK] = jax.nn.sigmoid(proj(base + c)).astype(BF16)
        gb_s[:, c:c + PROJ_BLK] = jax.nn.sigmoid(proj(base + D_MODEL + c)).astype(BF16)

    row_i = lax.broadcasted_iota(jnp.int32, (CHUNK, 2 * CHUNK), 0)
    col_i = lax.broadcasted_iota(jnp.int32, (CHUNK, 2 * CHUNK), 1)
    tril2 = ((col_i % CHUNK) <= row_i).astype(BF16)
    for c in range(n_chunks):
        rows = slice(c * CHUNK, (c + 1) * CHUNK)
        lf = a_s[rows, :]
        hi = lf.astype(BF16)
        lo = (lf - hi.astype(F32)).astype(BF16)
        a_c = jnp.dot(tril2, jnp.concatenate([hi, lo], axis=0), preferred_element_type=F32)
        a_s[rows, :] = a_c
        b_s[rows, :] = a_c - b_s[rows, :]
    for c in range(n_chunks):
        a_last = a_s[(c + 1) * CHUNK - 1:(c + 1) * CHUNK, :]
        for g in range(CHUNK // BF16_ROWS):
            r0 = c * CHUNK + g * BF16_ROWS
            rows = slice(r0, r0 + BF16_ROWS)
            a = a_s[rows, :]
            q = q_s[rows, :]
            qa_s[rows, :] = (q * jnp.exp2(a)).astype(BF16)
            kd_s[rows, :] = jnp.exp2(a_last - b_s[rows, :]).astype(BF16)
            for half in range(BF16_ROWS // SUB):
                rb = r0 + half * SUB
                rr = slice(rb, rb + SUB)
                if rb % CHUNK != 0:
                    qs_s[rr, :] = q_s[rr, :] * jnp.exp2(a_s[rr, :] - a_s[rb - 1:rb, :])

    lane_i = lax.broadcasted_iota(jnp.int32, (SUB, CHUNK), 1)
    subl_i = lax.broadcasted_iota(jnp.int32, (SUB, CHUNK), 0)

    for c in range(n_chunks):
        r0 = c * CHUNK
        rows = slice(r0, r0 + CHUNK)
        for h in range(A_HEADS):
            kl = slice(h * A_KDIM, (h + 1) * A_KDIM)
            vl = slice(h * A_VDIM, (h + 1) * A_VDIM)
            q = q_s[rows, kl]
            a = a_s[rows, kl]
            b = b_s[rows, kl]
            v = v_s[rows, vl]
            st = st_s[h]
            a_last = a_s[r0 + CHUNK - 1:r0 + CHUNK, kl]

            o = pl.dot(qa_s[rows, kl], st.astype(BF16), trans_b=True)

            blocks = []
            for i in range(N_SUB):
                r = slice(i * SUB, (i + 1) * SUB)
                q_i = q[r]
                a_i = a[r]
                acc = jnp.zeros((SUB, CHUNK), F32)
                for s in range(SUB):
                    rk = r0 + i * SUB + s
                    b_row = b_s[rk:rk + 1, kl]
                    col = jnp.sum(q_i * jnp.exp2(a_i - b_row), axis=-1, keepdims=True)
                    acc = jnp.where(lane_i == i * SUB + s, col, acc)
                blk = jnp.where(lane_i <= i * SUB + subl_i, acc, 0.0)
                if i > 0:
                    rb = r0 + i * SUB
                    anchor = a_s[rb - 1:rb, kl]
                    kp = jnp.exp2(anchor - b[0:i * SUB])
                    kp = jnp.concatenate([kp, jnp.zeros((CHUNK - i * SUB, A_KDIM), F32)], axis=0)
                    qs = qs_s[rb:rb + SUB, kl].astype(BF16)
                    blk = blk + pl.dot(qs, kp.astype(BF16), trans_b=True)
                blocks.append(blk)
            scores = jnp.concatenate(blocks, axis=0).astype(BF16)
            o = o + jnp.dot(scores, v, preferred_element_type=F32)

            ms = jnp.mean(o * o, axis=-1, keepdims=True)
            o = o * lax.rsqrt(ms + EPS) * hg_ref[:, vl] * og_s[rows, vl].astype(F32)
            oa_s[rows, vl] = o.astype(BF16)

            st_s[h] = st * jnp.exp2(a_last) + pl.dot(v, kd_s[rows, kl], trans_a=True)

    row_e = lax.broadcasted_iota(jnp.int32, (tm, B_GDIM), 0)
    pos = (t_idx * tm + row_e + pos0).astype(F32)
    for gi, w in enumerate(POOL_WINDOWS):
        gl = slice(gi * B_GDIM, (gi + 1) * B_GDIM)
        xs = pin_s[:, gl]
        acc = xs
        span = 1
        while span < w:
            acc = acc + pltpu.roll(acc, span, 0)
            span *= 2
        cnt = jnp.clip(pos, 1.0, float(w))
        d = acc[POOL_HALO:] / cnt - xs[POOL_HALO:]
        y = jnp.dot(d.astype(BF16), wpool_ref[gi], preferred_element_type=F32)
        ob_s[:, gl] = (y * pscale_ref[:, gl]).astype(BF16)
    pin_s[0:POOL_HALO, :] = pin_s[tm:tm + POOL_HALO, :]

    for j in range(D_MODEL // PROJ_BLK):
        cs = slice(j * PROJ_BLK, (j + 1) * PROJ_BLK)
        ya = jnp.dot(oa_s[...], wa_ref[:, cs], preferred_element_type=F32)
        yb = jnp.dot(ob_s[...], wb_ref[:, cs], preferred_element_type=F32)
        z_s[:, cs] = (ga_s[:, cs].astype(F32) * ya + gb_s[:, cs].astype(F32) * yb).astype(BF16)
    for j in range(D_MODEL // PROJ_BLK):
        cs = slice(j * PROJ_BLK, (j + 1) * PROJ_BLK)
        o_ref[0, :, cs] = h_ref[0, :, cs] + jnp.dot(z_s[...], wout_ref[:, cs], preferred_element_type=F32)

    if emit_carry:
        st_out_ref[...] = st_s[...]
        pin_out_ref[...] = pin_s[0:POOL_HALO, :]


def _ffn_kernel(tm, emit_carry, h_ref, halo0_ref, g2_ref, wup_ref, cw_ref, cb_ref, wdn_ref, gf_ref, *rest):
    if emit_carry:
        o_ref, halo_out_ref = rest[:2]
        rest = rest[2:]
    else:
        o_ref = rest[0]
        rest = rest[1:]
    z_s, u_s, halo_s, acc_s = rest
    t_idx = pl.program_id(1)

    @pl.when(t_idx == 0)
    def _():
        halo_s[...] = halo0_ref[...]

    z_s[...] = _rmsnorm(h_ref[0], g2_ref[...]).astype(BF16)
    for j in range(N_FF_BLK):
        cs = slice(j * FF_BLK, (j + 1) * FF_BLK)
        u = jnp.dot(z_s[...], wup_ref[:, cs], preferred_element_type=F32)
        v = jnp.dot(z_s[...], wup_ref[:, D_FF + j * FF_BLK:D_FF + (j + 1) * FF_BLK],
                    preferred_element_type=F32)
        u_s[0:CONV_HALO, :] = halo_s[j]
        u_s[CONV_HALO:, :] = u
        halo_s[j] = u_s[tm:tm + CONV_HALO, :]
        uc = cb_ref[:, cs] + cw_ref[CONV_W - 1:CONV_W, cs] * u
        for k in range(CONV_W - 1):
            lag = CONV_W - 1 - k
            uc = uc + cw_ref[k:k + 1, cs] * u_s[CONV_HALO - lag:CONV_HALO - lag + tm, :]
        g = (jax.nn.silu(uc) * v).astype(BF16)
        part = jnp.dot(g, wdn_ref[cs, :], preferred_element_type=F32)
        if j == 0:
            acc_s[...] = part
        else:
            acc_s[...] += part
    h2 = h_ref[0] + acc_s[...]
    o_ref[0] = _rmsnorm(h2, gf_ref[...])
    if emit_carry:
        halo_out_ref[...] = halo_s[...]


def _const_spec(shape):
    nd = len(shape)
    return pl.BlockSpec(shape, lambda b, t: (0,) * nd, pipeline_mode=pl.Buffered(1))


def _mixer_call(h, st0, pin0, consts, tm, pos0, emit_carry):
    nb, t_len, d = h.shape
    tile_spec = pl.BlockSpec((1, tm, d), lambda b, t: (b, t, 0))
    ins = [h, st0, pin0] + consts
    out_shape = [jax.ShapeDtypeStruct((nb, t_len, d), F32)]
    out_specs = [tile_spec]
    if emit_carry:
        out_shape += [jax.ShapeDtypeStruct(st0.shape, F32), jax.ShapeDtypeStruct(pin0.shape, F32)]
        out_specs += [pl.BlockSpec(st0.shape, lambda b, t: (0, 0, 0)), pl.BlockSpec(pin0.shape, lambda b, t: (0, 0))]
    return pl.pallas_call(
        functools.partial(_mixer_kernel, tm, pos0, emit_carry),
        grid=(nb, t_len // tm),
        in_specs=[tile_spec] + [_const_spec(a.shape) for a in ins[1:]],
        out_specs=out_specs,
        out_shape=out_shape,
        scratch_shapes=[
            pltpu.VMEM((tm, D_MODEL), BF16),
            pltpu.VMEM((tm, A_FDIM), F32),
            pltpu.VMEM((tm, A_FDIM), F32),
            pltpu.VMEM((tm, A_FDIM), F32),
            pltpu.VMEM((tm, A_FDIM), BF16),
            pltpu.VMEM((tm, A_FDIM), F32),
            pltpu.VMEM((tm, A_FDIM), BF16),
            pltpu.VMEM((tm, A_WIDTH), BF16),
            pltpu.VMEM((tm, A_WIDTH), BF16),
            pltpu.VMEM((POOL_HALO + tm, B_WIDTH), F32),
            pltpu.VMEM((tm, D_MODEL), BF16),
            pltpu.VMEM((tm, D_MODEL), BF16),
            pltpu.VMEM((tm, A_WIDTH), BF16),
            pltpu.VMEM((tm, B_WIDTH), BF16),
            pltpu.VMEM((A_HEADS, A_VDIM, A_KDIM), F32),
        ],
        compiler_params=pltpu.CompilerParams(dimension_semantics=("arbitrary", "arbitrary"),
                                             vmem_limit_bytes=VMEM_LIMIT_BYTES),
        name="mixer_meta" if emit_carry else "mixer",
    )(*ins)


def _ffn_call(h, halo0, consts, tm, emit_carry):
    nb, t_len, d = h.shape
    tile_spec = pl.BlockSpec((1, tm, d), lambda b, t: (b, t, 0))
    ins = [h, halo0] + consts
    out_shape = [jax.ShapeDtypeStruct((nb, t_len, d), F32)]
    out_specs = [tile_spec]
    if emit_carry:
        out_shape += [jax.ShapeDtypeStruct(halo0.shape, F32)]
        out_specs += [pl.BlockSpec(halo0.shape, lambda b, t: (0, 0, 0))]
    return pl.pallas_call(
        functools.partial(_ffn_kernel, tm, emit_carry),
        grid=(nb, t_len // tm),
        in_specs=[tile_spec] + [_const_spec(a.shape) for a in ins[1:]],
        out_specs=out_specs,
        out_shape=out_shape,
        scratch_shapes=[
            pltpu.VMEM((tm, D_MODEL), BF16),
            pltpu.VMEM((CONV_HALO + tm, FF_BLK), F32),
            pltpu.VMEM((N_FF_BLK, CONV_HALO, FF_BLK), F32),
            pltpu.VMEM((tm, D_MODEL), F32),
        ],
        compiler_params=pltpu.CompilerParams(dimension_semantics=("arbitrary", "arbitrary"),
                                             vmem_limit_bytes=VMEM_LIMIT_BYTES),
        name="ffn_meta" if emit_carry else "ffn",
    )(*ins)


def kernel(x, meta_tokens, lb_logits, norm1_g, w_in, b_f, head_norm_g, w_pool, pool_scale, w_branch_a,
           w_branch_b, w_out, norm2_g, w_up, conv_w, conv_b, w_down, final_norm_g):
    B, S, D = x.shape
    assert D == D_MODEL and S % MAIN_TM == 0 and meta_tokens.shape[0] == N_META

    row = lambda p: p.reshape(1, -1).astype(F32)
    mixer_consts = [
        lb_logits.astype(F32), row(norm1_g[0]), w_in[0].astype(BF16), row(b_f[0]), row(head_norm_g[0]),
        w_pool[0].astype(BF16), row(pool_scale[0]), w_branch_a[0].astype(BF16), w_branch_b[0].astype(BF16),
        w_out[0].astype(BF16),
    ]
    ffn_consts = [
        row(norm2_g[0]), w_up[0].astype(BF16), conv_w[0].astype(F32), row(conv_b[0]),
        w_down[0].astype(BF16), row(final_norm_g),
    ]

    hm = jnp.concatenate([jnp.zeros((PAD, D), x.dtype), meta_tokens.astype(x.dtype)], axis=0)[None]
    st0 = jnp.zeros((A_HEADS, A_VDIM, A_KDIM), F32)
    pin0 = jnp.zeros((POOL_HALO, B_WIDTH), F32)
    halo0 = jnp.zeros((N_FF_BLK, CONV_HALO, FF_BLK), F32)
    h1m, st_m, pin_m = _mixer_call(hm, st0, pin0, mixer_consts, CHUNK, 1 - PAD, True)
    _, halo_m = _ffn_call(h1m, halo0, ffn_consts, CHUNK, True)

    (h1,) = _mixer_call(x, st_m, pin_m, mixer_consts, MAIN_TM, N_META + 1, False)
    (out,) = _ffn_call(h1, halo_m, ffn_consts, MAIN_TM, False)
    return out
```

```python
import functools

import jax
import jax.numpy as jnp
from jax import lax
from jax.experimental import pallas as pl
from jax.experimental.pallas import tpu as pltpu

D_MODEL = 1024
CHUNK = 64
N_META = 16
PAD = CHUNK - N_META
A_KDIM = 128
A_HEADS = D_MODEL // A_KDIM
A_FDIM = A_HEADS * A_KDIM
A_WIDTH = D_MODEL // 2
A_VDIM = A_WIDTH // A_HEADS
POOL_WINDOWS = (2, 4, 8, 16)
B_GROUPS = 4
B_WIDTH = D_MODEL // 2
B_GDIM = B_WIDTH // B_GROUPS
D_FF = ((8 * D_MODEL // 3 + 127) // 128) * 128
CONV_W = 3
EPS = 1e-6

SUB = 8
N_SUB = CHUNK // SUB
BF16_ROWS = 16
POOL_HALO = 16
CONV_HALO = 8
MAIN_TM = 8 * CHUNK
PROJ_BLK = 256
N_PROJ_BLK = (2 * A_FDIM + 2 * A_WIDTH + B_WIDTH + 2 * D_MODEL) // PROJ_BLK
FF_BLK = 256
N_FF_BLK = D_FF // FF_BLK
VMEM_LIMIT_BYTES = 60000 * 1024

F32 = jnp.float32
BF16 = jnp.bfloat16


def _rmsnorm(x, g):
    ms = jnp.mean(x * x, axis=-1, keepdims=True)
    return x * lax.rsqrt(ms + EPS) * g


def _mixer_kernel(tm, n_t, pos0, emit_carry,
                  x_ref, st0_ref, pin0_ref, lbl_ref, g1_ref, win_ref, bf_ref, hg_ref, wpool_ref, pscale_ref,
                  wa_ref, wb_ref, wout_ref, *rest):
    if emit_carry:
        o_ref, st_out_ref, pin_out_ref = rest[:3]
        rest = rest[3:]
    else:
        o_ref = rest[0]
        rest = rest[1:]
    (z_s, mix_s, oa_s, ob_s, st_s, halo_s, q_s, a_s, b_s, qa_s, kd_s, v_s, og_s, pin_s, ga_s, gb_s) = rest
    slotted = (q_s, a_s, b_s, qa_s, kd_s, v_s, og_s, pin_s, ga_s, gb_s)
    n_chunks = tm // CHUNK
    g = pl.program_id(0)
    p = g % 2
    r = 1 - p
    tr = jnp.maximum(g - 1, 0) % n_t

    @pl.when(g == 0)
    def _():
        for s in slotted:
            s[1] = jnp.zeros(s.shape[1:], s.dtype)

    @pl.when(tr == 0)
    def _():
        st_s[...] = st0_ref[...]
        halo_s[...] = pin0_ref[...]

    def norm_in():
        z_s[...] = _rmsnorm(x_ref[0], g1_ref[...]).astype(BF16)

    def proj(col0):
        return jnp.dot(z_s[...], win_ref[:, col0:col0 + PROJ_BLK], preferred_element_type=F32)

    def proj_block(k):
        col0 = k * PROJ_BLK
        if col0 < A_FDIM:
            q_s[p, :, col0:col0 + PROJ_BLK] = jax.nn.silu(proj(col0))
        elif col0 < 2 * A_FDIM:
            c = col0 - A_FDIM
            l0 = lbl_ref[0:1, c:c + PROJ_BLK]
            l1 = lbl_ref[1:2, c:c + PROJ_BLK]
            lmax = jnp.maximum(l0, l1)
            e0 = jnp.exp(l0 - lmax)
            e1 = jnp.exp(l1 - lmax)
            lb = e0 / (e0 + e1)
            sig = jax.nn.sigmoid(proj(col0) + bf_ref[:, c:c + PROJ_BLK])
            a_s[p, :, c:c + PROJ_BLK] = jnp.log2(lb + (1.0 - lb) * sig)
            b_s[p, :, c:c + PROJ_BLK] = jnp.log2((1.0 - lb) * (1.0 - sig))
        elif col0 < 2 * A_FDIM + A_WIDTH:
            c = col0 - 2 * A_FDIM
            v_s[p, :, c:c + PROJ_BLK] = proj(col0).astype(BF16)
        elif col0 < 2 * A_FDIM + 2 * A_WIDTH:
            c = col0 - (2 * A_FDIM + A_WIDTH)
            og_s[p, :, c:c + PROJ_BLK] = jax.nn.silu(proj(col0)).astype(BF16)
        elif col0 < 2 * A_FDIM + 2 * A_WIDTH + B_WIDTH:
            c = col0 - (2 * A_FDIM + 2 * A_WIDTH)
            pin_s[p, :, c:c + PROJ_BLK] = proj(col0)
        elif col0 < 2 * A_FDIM + 2 * A_WIDTH + B_WIDTH + D_MODEL:
            c = col0 - (2 * A_FDIM + 2 * A_WIDTH + B_WIDTH)
            ga_s[p, :, c:c + PROJ_BLK] = jax.nn.sigmoid(proj(col0)).astype(BF16)
        else:
            c = col0 - (2 * A_FDIM + 2 * A_WIDTH + B_WIDTH + D_MODEL)
            gb_s[p, :, c:c + PROJ_BLK] = jax.nn.sigmoid(proj(col0)).astype(BF16)

    row_i = lax.broadcasted_iota(jnp.int32, (CHUNK, 2 * CHUNK), 0)
    col_i = lax.broadcasted_iota(jnp.int32, (CHUNK, 2 * CHUNK), 1)
    tril2 = ((col_i % CHUNK) <= row_i).astype(BF16)

    def decay_chunk(c):
        rows = slice(c * CHUNK, (c + 1) * CHUNK)
        lf = a_s[p, rows, :]
        hi = lf.astype(BF16)
        lo = (lf - hi.astype(F32)).astype(BF16)
        a_c = jnp.dot(tril2, jnp.concatenate([hi, lo], axis=0), preferred_element_type=F32)
        b_c = a_c - b_s[p, rows, :]
        a_s[p, rows, :] = a_c
        b_s[p, rows, :] = b_c
        a_last = a_c[CHUNK - 1:CHUNK, :]
        qa_s[p, rows, :] = (q_s[p, rows, :] * jnp.exp2(a_c)).astype(BF16)
        kd_s[p, rows, :] = jnp.exp2(a_last - b_c).astype(BF16)

    lane_i = lax.broadcasted_iota(jnp.int32, (SUB, CHUNK), 1)
    subl_i = lax.broadcasted_iota(jnp.int32, (SUB, CHUNK), 0)

    def hgrn_stages(c):
        r0 = c * CHUNK
        rows = slice(r0, r0 + CHUNK)
        kls = [slice(h * A_KDIM, (h + 1) * A_KDIM) for h in range(A_HEADS)]
        vls = [slice(h * A_VDIM, (h + 1) * A_VDIM) for h in range(A_HEADS)]
        diag, qs, kp, o_inter, off, upd, scores, intra = ({} for _ in range(8))

        def operands():
            for h in range(A_HEADS):
                kl = kls[h]
                q = q_s[r, rows, kl]
                a = a_s[r, rows, kl]
                b = b_s[r, rows, kl]
                for i in range(N_SUB):
                    rs = slice(i * SUB, (i + 1) * SUB)
                    q_i = q[rs]
                    a_i = a[rs]
                    acc = jnp.zeros((SUB, CHUNK), F32)
                    for s in range(SUB):
                        rk = r0 + i * SUB + s
                        b_row = b_s[r, rk:rk + 1, kl]
                        col = jnp.sum(q_i * jnp.exp2(a_i - b_row), axis=-1, keepdims=True)
                        acc = jnp.where(lane_i == i * SUB + s, col, acc)
                    diag[h, i] = jnp.where(lane_i <= i * SUB + subl_i, acc, 0.0)
                    if i > 0:
                        rb = r0 + i * SUB
                        anchor = a_s[r, rb - 1:rb, kl]
                        qs[h, i] = (q_i * jnp.exp2(a_i - anchor)).astype(BF16)
                        kpi = jnp.exp2(anchor - b[0:i * SUB])
                        kpi = jnp.concatenate([kpi, jnp.zeros((CHUNK - i * SUB, A_KDIM), F32)], axis=0)
                        kp[h, i] = kpi.astype(BF16)

        def small_matmuls():
            for h in range(A_HEADS):
                kl = kls[h]
                o_inter[h] = pl.dot(qa_s[r, rows, kl], st_s[h].astype(BF16), trans_b=True)
                for i in range(1, N_SUB):
                    off[h, i] = pl.dot(qs[h, i], kp[h, i], trans_b=True)
                upd[h] = pl.dot(v_s[r, rows, vls[h]], kd_s[r, rows, kl], trans_a=True)

        def assemble():
            for h in range(A_HEADS):
                blocks = [diag[h, 0]] + [diag[h, i] + off[h, i] for i in range(1, N_SUB)]
                scores[h] = jnp.concatenate(blocks, axis=0).astype(BF16)
                a_last = a_s[r, r0 + CHUNK - 1:r0 + CHUNK, kls[h]]
                st_s[h] = st_s[h] * jnp.exp2(a_last) + upd[h]

        def intra_matmuls():
            for h in range(A_HEADS):
                intra[h] = jnp.dot(scores[h], v_s[r, rows, vls[h]], preferred_element_type=F32)

        def finish():
            for h in range(A_HEADS):
                vl = vls[h]
                o = o_inter[h] + intra[h]
                ms = jnp.mean(o * o, axis=-1, keepdims=True)
                o = o * lax.rsqrt(ms + EPS) * hg_ref[:, vl] * og_s[r, rows, vl].astype(F32)
                oa_s[rows, vl] = o.astype(BF16)

        return [operands, small_matmuls, assemble, intra_matmuls, finish]

    def pool_group(gi):
        w = POOL_WINDOWS[gi]
        row_e = lax.broadcasted_iota(jnp.int32, (tm, B_GDIM), 0)
        pos = (tr * tm + row_e + pos0).astype(F32)
        if True:
            gl = slice(gi * B_GDIM, (gi + 1) * B_GDIM)
            cur = pin_s[r, :, gl]
            xs = jnp.concatenate([halo_s[:, gl], cur], axis=0)
            acc = xs
            span = 1
            while span < w:
                acc = acc + pltpu.roll(acc, span, 0)
                span *= 2
            cnt = jnp.clip(pos, 1.0, float(w))
            d = acc[POOL_HALO:] / cnt - cur
            y = jnp.dot(d.astype(BF16), wpool_ref[gi], preferred_element_type=F32)
            ob_s[:, gl] = (y * pscale_ref[:, gl]).astype(BF16)

    def pool_carry():
        halo_s[...] = pin_s[r, tm - POOL_HALO:tm, :]

    def merge_block(j):
        cs = slice(j * PROJ_BLK, (j + 1) * PROJ_BLK)
        ya = jnp.dot(oa_s[...], wa_ref[:, cs], preferred_element_type=F32)
        yb = jnp.dot(ob_s[...], wb_ref[:, cs], preferred_element_type=F32)
        mix_s[:, cs] = (ga_s[r, :, cs].astype(F32) * ya + gb_s[r, :, cs].astype(F32) * yb).astype(BF16)

    def out_block(j):
        cs = slice(j * PROJ_BLK, (j + 1) * PROJ_BLK)
        o_ref[0, :, cs] = jnp.dot(mix_s[...], wout_ref[:, cs], preferred_element_type=F32)

    P = functools.partial
    n_merge = D_MODEL // PROJ_BLK
    gap = None
    stages = [hgrn_stages(c) for c in range(n_chunks)]
    pools = [P(pool_group, gi) for gi in range(B_GROUPS)] + [pool_carry]
    seq_a = [stages[0][0]]
    for c in range(n_chunks):
        _, small_matmuls, assemble, intra_matmuls, finish = stages[c]
        seq_a += [small_matmuls, gap]
        if c + 1 < n_chunks:
            seq_a.append(stages[c + 1][0])
        seq_a += [assemble, intra_matmuls, gap, finish, gap]
        if pools:
            seq_a.append(pools.pop(0))
    seq_a += pools
    heavy_a = [P(proj_block, k) for k in range(N_PROJ_BLK)]
    seq_b = [x for c in range(n_chunks) for x in (P(decay_chunk, c), gap)]
    heavy_b = [P(merge_block, j) for j in range(n_merge)] + [P(out_block, j) for j in range(n_merge)]
    norm_in()
    for seq, heavy in ((seq_a, heavy_a), (seq_b, heavy_b)):
        n_gaps = seq.count(gap)
        seen = 0
        for item in seq:
            if item is gap:
                for k in range(seen * len(heavy) // n_gaps, (seen + 1) * len(heavy) // n_gaps):
                    heavy[k]()
                seen += 1
            else:
                item()

    if emit_carry:
        st_out_ref[...] = st_s[...]
        pin_out_ref[...] = halo_s[...]


def _ffn_kernel(tm, emit_carry, x_ref, d_ref, halo0_ref, g2_ref, wup_ref, cw_ref, cb_ref, wdn_ref, gf_ref, *rest):
    if emit_carry:
        o_ref, halo_out_ref = rest[:2]
        rest = rest[2:]
    else:
        o_ref = rest[0]
        rest = rest[1:]
    z_s, u_s, halo_s, acc_s = rest
    t_idx = pl.program_id(1)

    @pl.when(t_idx == 0)
    def _():
        halo_s[...] = halo0_ref[...]

    z_s[...] = _rmsnorm(x_ref[0] + d_ref[0], g2_ref[...]).astype(BF16)

    def up(j):
        u = jnp.dot(z_s[...], wup_ref[:, j * FF_BLK:(j + 1) * FF_BLK], preferred_element_type=F32)
        v = jnp.dot(z_s[...], wup_ref[:, D_FF + j * FF_BLK:D_FF + (j + 1) * FF_BLK],
                    preferred_element_type=F32)
        return u, v

    nxt = up(0)
    for j in range(N_FF_BLK):
        cs = slice(j * FF_BLK, (j + 1) * FF_BLK)
        u, v = nxt
        if j + 1 < N_FF_BLK:
            nxt = up(j + 1)
        u_s[0:CONV_HALO, :] = halo_s[j]
        u_s[CONV_HALO:, :] = u
        halo_s[j] = u_s[tm:tm + CONV_HALO, :]
        uc = cb_ref[:, cs] + cw_ref[CONV_W - 1:CONV_W, cs] * u
        for k in range(CONV_W - 1):
            lag = CONV_W - 1 - k
            uc = uc + cw_ref[k:k + 1, cs] * u_s[CONV_HALO - lag:CONV_HALO - lag + tm, :]
        gated = (jax.nn.silu(uc) * v).astype(BF16)
        part = jnp.dot(gated, wdn_ref[cs, :], preferred_element_type=F32)
        if j == 0:
            acc_s[...] = part
        else:
            acc_s[...] += part
    h2 = x_ref[0] + d_ref[0] + acc_s[...]
    o_ref[0] = _rmsnorm(h2, gf_ref[...])
    if emit_carry:
        halo_out_ref[...] = halo_s[...]


def _mixer_call(x, st0, pin0, consts, tm, pos0, emit_carry):
    nb, t_len, d = x.shape
    n_t = t_len // tm
    n_tiles = nb * n_t

    def tile_map(shift):
        def index_map(g):
            tile = jnp.clip(g - shift, 0, n_tiles - 1)
            return (tile // n_t, tile % n_t, 0)
        return index_map

    def const_spec(shape):
        nd = len(shape)
        return pl.BlockSpec(shape, lambda g: (0,) * nd, pipeline_mode=pl.Buffered(1))

    ins = [x, st0, pin0] + consts
    out_shape = [jax.ShapeDtypeStruct((nb, t_len, d), F32)]
    out_specs = [pl.BlockSpec((1, tm, d), tile_map(1))]
    if emit_carry:
        out_shape += [jax.ShapeDtypeStruct(st0.shape, F32), jax.ShapeDtypeStruct(pin0.shape, F32)]
        out_specs += [pl.BlockSpec(st0.shape, lambda g: (0, 0, 0)), pl.BlockSpec(pin0.shape, lambda g: (0, 0))]
    slot = lambda cols, dtype: pltpu.VMEM((2, tm, cols), dtype)
    return pl.pallas_call(
        functools.partial(_mixer_kernel, tm, n_t, pos0, emit_carry),
        grid=(n_tiles + 1,),
        in_specs=[pl.BlockSpec((1, tm, d), tile_map(0))] + [const_spec(a.shape) for a in ins[1:]],
        out_specs=out_specs,
        out_shape=out_shape,
        scratch_shapes=[
            pltpu.VMEM((tm, D_MODEL), BF16),
            pltpu.VMEM((tm, D_MODEL), BF16),
            pltpu.VMEM((tm, A_WIDTH), BF16),
            pltpu.VMEM((tm, B_WIDTH), BF16),
            pltpu.VMEM((A_HEADS, A_VDIM, A_KDIM), F32),
            pltpu.VMEM((POOL_HALO, B_WIDTH), F32),
            slot(A_FDIM, F32),
            slot(A_FDIM, F32),
            slot(A_FDIM, F32),
            slot(A_FDIM, BF16),
            slot(A_FDIM, BF16),
            slot(A_WIDTH, BF16),
            slot(A_WIDTH, BF16),
            slot(B_WIDTH, F32),
            slot(D_MODEL, BF16),
            slot(D_MODEL, BF16),
        ],
        compiler_params=pltpu.CompilerParams(dimension_semantics=("arbitrary",),
                                             vmem_limit_bytes=VMEM_LIMIT_BYTES),
        name="mixer_meta" if emit_carry else "mixer",
    )(*ins)


def _ffn_call(x, delta, halo0, consts, tm, emit_carry):
    nb, t_len, d = x.shape
    tile_spec = pl.BlockSpec((1, tm, d), lambda b, t: (b, t, 0))

    def const_spec(shape):
        nd = len(shape)
        return pl.BlockSpec(shape, lambda b, t: (0,) * nd, pipeline_mode=pl.Buffered(1))

    ins = [x, delta, halo0] + consts
    out_shape = [jax.ShapeDtypeStruct((nb, t_len, d), F32)]
    out_specs = [tile_spec]
    if emit_carry:
        out_shape += [jax.ShapeDtypeStruct(halo0.shape, F32)]
        out_specs += [pl.BlockSpec(halo0.shape, lambda b, t: (0, 0, 0))]
    return pl.pallas_call(
        functools.partial(_ffn_kernel, tm, emit_carry),
        grid=(nb, t_len // tm),
        in_specs=[tile_spec, tile_spec] + [const_spec(a.shape) for a in ins[2:]],
        out_specs=out_specs,
        out_shape=out_shape,
        scratch_shapes=[
            pltpu.VMEM((tm, D_MODEL), BF16),
            pltpu.VMEM((CONV_HALO + tm, FF_BLK), F32),
            pltpu.VMEM((N_FF_BLK, CONV_HALO, FF_BLK), F32),
            pltpu.VMEM((tm, D_MODEL), F32),
        ],
        compiler_params=pltpu.CompilerParams(dimension_semantics=("arbitrary", "arbitrary"),
                                             vmem_limit_bytes=VMEM_LIMIT_BYTES),
        name="ffn_meta" if emit_carry else "ffn",
    )(*ins)


def kernel(x, meta_tokens, lb_logits, norm1_g, w_in, b_f, head_norm_g, w_pool, pool_scale, w_branch_a,
           w_branch_b, w_out, norm2_g, w_up, conv_w, conv_b, w_down, final_norm_g):
    B, S, D = x.shape
    assert D == D_MODEL and S % MAIN_TM == 0 and meta_tokens.shape[0] == N_META

    row = lambda p: p.reshape(1, -1).astype(F32)
    mixer_consts = [
        lb_logits.astype(F32), row(norm1_g[0]), w_in[0].astype(BF16), row(b_f[0]), row(head_norm_g[0]),
        w_pool[0].astype(BF16), row(pool_scale[0]), w_branch_a[0].astype(BF16), w_branch_b[0].astype(BF16),
        w_out[0].astype(BF16),
    ]
    ffn_consts = [
        row(norm2_g[0]), w_up[0].astype(BF16), conv_w[0].astype(F32), row(conv_b[0]),
        w_down[0].astype(BF16), row(final_norm_g),
    ]

    xm = jnp.concatenate([jnp.zeros((PAD, D), x.dtype), meta_tokens.astype(x.dtype)], axis=0)[None]
    st0 = jnp.zeros((A_HEADS, A_VDIM, A_KDIM), F32)
    pin0 = jnp.zeros((POOL_HALO, B_WIDTH), F32)
    halo0 = jnp.zeros((N_FF_BLK, CONV_HALO, FF_BLK), F32)
    dm, st_m, pin_m = _mixer_call(xm, st0, pin0, mixer_consts, CHUNK, 1 - PAD, True)
    _, halo_m = _ffn_call(xm, dm, halo0, ffn_consts, CHUNK, True)

    (delta,) = _mixer_call(x, st_m, pin_m, mixer_consts, MAIN_TM, N_META + 1, False)
    (out,) = _ffn_call(x, delta, halo_m, ffn_consts, MAIN_TM, False)
    return out
```

```python
import functools

import jax
import jax.numpy as jnp
from jax import lax
from jax.experimental import pallas as pl
from jax.experimental.pallas import tpu as pltpu

D_MODEL = 1024
CHUNK = 64
N_META = 16
PAD = CHUNK - N_META
A_KDIM = 128
A_HEADS = D_MODEL // A_KDIM
A_FDIM = A_HEADS * A_KDIM
A_WIDTH = D_MODEL // 2
A_VDIM = A_WIDTH // A_HEADS
POOL_WINDOWS = (2, 4, 8, 16)
B_GROUPS = 4
B_WIDTH = D_MODEL // 2
B_GDIM = B_WIDTH // B_GROUPS
D_FF = ((8 * D_MODEL // 3 + 127) // 128) * 128
CONV_W = 3
EPS = 1e-6

SUB = 8
N_SUB = CHUNK // SUB
BF16_ROWS = 16
POOL_HALO = 16
CONV_HALO = 8
MAIN_TM = 8 * CHUNK
PROJ_BLK = 256
N_PROJ_BLK = (2 * A_FDIM + 2 * A_WIDTH + B_WIDTH + 2 * D_MODEL) // PROJ_BLK
FF_BLK = 256
N_FF_BLK = D_FF // FF_BLK
VMEM_LIMIT_BYTES = 60000 * 1024

F32 = jnp.float32
BF16 = jnp.bfloat16


def _rmsnorm(x, g):
    ms = jnp.mean(x * x, axis=-1, keepdims=True)
    return x * lax.rsqrt(ms + EPS) * g


def _mixer_kernel(tm, n_t, pos0, emit_carry,
                  x_ref, st0_ref, pin0_ref, lbl_ref, g1_ref, win_ref, bf_ref, hg_ref, wpool_ref, pscale_ref,
                  wa_ref, wb_ref, wout_ref, *rest):
    if emit_carry:
        o_ref, st_out_ref, pin_out_ref = rest[:3]
        rest = rest[3:]
    else:
        o_ref = rest[0]
        rest = rest[1:]
    (z_s, mix_s, oa_s, ob_s, st_s, halo_s, q_s, a_s, b_s, qa_s, kd_s, v_s, og_s, pin_s, ga_s, gb_s) = rest
    slotted = (q_s, a_s, b_s, qa_s, kd_s, v_s, og_s, pin_s, ga_s, gb_s)
    n_chunks = tm // CHUNK
    g = pl.program_id(0)
    p = g % 2
    r = 1 - p
    tr = jnp.maximum(g - 1, 0) % n_t

    @pl.when(g == 0)
    def _():
        for s in slotted:
            s[1] = jnp.zeros(s.shape[1:], s.dtype)

    @pl.when(tr == 0)
    def _():
        st_s[...] = st0_ref[...]
        halo_s[...] = pin0_ref[...]

    def norm_in():
        z_s[...] = _rmsnorm(x_ref[0], g1_ref[...]).astype(BF16)

    def proj(col0):
        return jnp.dot(z_s[...], win_ref[:, col0:col0 + PROJ_BLK], preferred_element_type=F32)

    def proj_block(k):
        col0 = k * PROJ_BLK
        if col0 < A_FDIM:
            q_s[p, :, col0:col0 + PROJ_BLK] = jax.nn.silu(proj(col0))
        elif col0 < 2 * A_FDIM:
            c = col0 - A_FDIM
            l0 = lbl_ref[0:1, c:c + PROJ_BLK]
            l1 = lbl_ref[1:2, c:c + PROJ_BLK]
            lmax = jnp.maximum(l0, l1)
            e0 = jnp.exp(l0 - lmax)
            e1 = jnp.exp(l1 - lmax)
            lb = e0 / (e0 + e1)
            sig = jax.nn.sigmoid(proj(col0) + bf_ref[:, c:c + PROJ_BLK])
            a_s[p, :, c:c + PROJ_BLK] = jnp.log2(lb + (1.0 - lb) * sig)
            b_s[p, :, c:c + PROJ_BLK] = jnp.log2((1.0 - lb) * (1.0 - sig))
        elif col0 < 2 * A_FDIM + A_WIDTH:
            c = col0 - 2 * A_FDIM
            v_s[p, :, c:c + PROJ_BLK] = proj(col0).astype(BF16)
        elif col0 < 2 * A_FDIM + 2 * A_WIDTH:
            c = col0 - (2 * A_FDIM + A_WIDTH)
            og_s[p, :, c:c + PROJ_BLK] = jax.nn.silu(proj(col0)).astype(BF16)
        elif col0 < 2 * A_FDIM + 2 * A_WIDTH + B_WIDTH:
            c = col0 - (2 * A_FDIM + 2 * A_WIDTH)
            pin_s[p, :, c:c + PROJ_BLK] = proj(col0)
        elif col0 < 2 * A_FDIM + 2 * A_WIDTH + B_WIDTH + D_MODEL:
            c = col0 - (2 * A_FDIM + 2 * A_WIDTH + B_WIDTH)
            ga_s[p, :, c:c + PROJ_BLK] = jax.nn.sigmoid(proj(col0)).astype(BF16)
        else:
            c = col0 - (2 * A_FDIM + 2 * A_WIDTH + B_WIDTH + D_MODEL)
            gb_s[p, :, c:c + PROJ_BLK] = jax.nn.sigmoid(proj(col0)).astype(BF16)

    row_i = lax.broadcasted_iota(jnp.int32, (CHUNK, 2 * CHUNK), 0)
    col_i = lax.broadcasted_iota(jnp.int32, (CHUNK, 2 * CHUNK), 1)
    tril2 = ((col_i % CHUNK) <= row_i).astype(BF16)

    def decay_chunk(c):
        rows = slice(c * CHUNK, (c + 1) * CHUNK)
        lf = a_s[p, rows, :]
        hi = lf.astype(BF16)
        lo = (lf - hi.astype(F32)).astype(BF16)
        a_c = jnp.dot(tril2, jnp.concatenate([hi, lo], axis=0), preferred_element_type=F32)
        b_c = a_c - b_s[p, rows, :]
        a_s[p, rows, :] = a_c
        b_s[p, rows, :] = b_c
        a_last = a_c[CHUNK - 1:CHUNK, :]
        qa_s[p, rows, :] = (q_s[p, rows, :] * jnp.exp2(a_c)).astype(BF16)
        kd_s[p, rows, :] = jnp.exp2(a_last - b_c).astype(BF16)

    lane_i = lax.broadcasted_iota(jnp.int32, (SUB, CHUNK), 1)
    subl_i = lax.broadcasted_iota(jnp.int32, (SUB, CHUNK), 0)

    def hgrn_stages(c):
        r0 = c * CHUNK
        rows = slice(r0, r0 + CHUNK)
        kls = [slice(h * A_KDIM, (h + 1) * A_KDIM) for h in range(A_HEADS)]
        vls = [slice(h * A_VDIM, (h + 1) * A_VDIM) for h in range(A_HEADS)]
        diag, qs, kp, o_inter, off, upd, scores, intra = ({} for _ in range(8))

        def operands():
            for h in range(A_HEADS):
                kl = kls[h]
                q = q_s[r, rows, kl]
                a = a_s[r, rows, kl]
                b = b_s[r, rows, kl]
                for i in range(N_SUB):
                    rs = slice(i * SUB, (i + 1) * SUB)
                    q_i = q[rs]
                    a_i = a[rs]
                    acc = jnp.zeros((SUB, CHUNK), F32)
                    for s in range(SUB):
                        rk = r0 + i * SUB + s
                        b_row = b_s[r, rk:rk + 1, kl]
                        col = jnp.sum(q_i * jnp.exp2(a_i - b_row), axis=-1, keepdims=True)
                        acc = jnp.where(lane_i == i * SUB + s, col, acc)
                    diag[h, i] = jnp.where(lane_i <= i * SUB + subl_i, acc, 0.0)
                    if i > 0:
                        rb = r0 + i * SUB
                        anchor = a_s[r, rb - 1:rb, kl]
                        qs[h, i] = (q_i * jnp.exp2(a_i - anchor)).astype(BF16)
                        kpi = jnp.exp2(anchor - b[0:i * SUB])
                        kpi = jnp.concatenate([kpi, jnp.zeros((CHUNK - i * SUB, A_KDIM), F32)], axis=0)
                        kp[h, i] = kpi.astype(BF16)

        def small_matmuls():
            for h in range(A_HEADS):
                kl = kls[h]
                o_inter[h] = pl.dot(qa_s[r, rows, kl], st_s[h].astype(BF16), trans_b=True)
                for i in range(1, N_SUB):
                    off[h, i] = pl.dot(qs[h, i], kp[h, i], trans_b=True)
                upd[h] = pl.dot(v_s[r, rows, vls[h]], kd_s[r, rows, kl], trans_a=True)

        def assemble():
            for h in range(A_HEADS):
                blocks = [diag[h, 0]] + [diag[h, i] + off[h, i] for i in range(1, N_SUB)]
                scores[h] = jnp.concatenate(blocks, axis=0).astype(BF16)
                a_last = a_s[r, r0 + CHUNK - 1:r0 + CHUNK, kls[h]]
                st_s[h] = st_s[h] * jnp.exp2(a_last) + upd[h]

        def intra_matmuls():
            for h in range(A_HEADS):
                intra[h] = jnp.dot(scores[h], v_s[r, rows, vls[h]], preferred_element_type=F32)

        def finish():
            for h in range(A_HEADS):
                vl = vls[h]
                o = o_inter[h] + intra[h]
                ms = jnp.mean(o * o, axis=-1, keepdims=True)
                o = o * lax.rsqrt(ms + EPS) * hg_ref[:, vl] * og_s[r, rows, vl].astype(F32)
                oa_s[rows, vl] = o.astype(BF16)

        return [operands, small_matmuls, assemble, intra_matmuls, finish]

    def pool_group(gi):
        w = POOL_WINDOWS[gi]
        row_e = lax.broadcasted_iota(jnp.int32, (tm, B_GDIM), 0)
        pos = (tr * tm + row_e + pos0).astype(F32)
        if True:
            gl = slice(gi * B_GDIM, (gi + 1) * B_GDIM)
            cur = pin_s[r, :, gl]
            xs = jnp.concatenate([halo_s[:, gl], cur], axis=0)
            acc = xs
            span = 1
            while span < w:
                acc = acc + pltpu.roll(acc, span, 0)
                span *= 2
            cnt = jnp.clip(pos, 1.0, float(w))
            d = acc[POOL_HALO:] / cnt - cur
            y = jnp.dot(d.astype(BF16), wpool_ref[gi], preferred_element_type=F32)
            ob_s[:, gl] = (y * pscale_ref[:, gl]).astype(BF16)

    def pool_carry():
        halo_s[...] = pin_s[r, tm - POOL_HALO:tm, :]

    def merge_block(j):
        cs = slice(j * PROJ_BLK, (j + 1) * PROJ_BLK)
        ya = jnp.dot(oa_s[...], wa_ref[:, cs], preferred_element_type=F32)
        yb = jnp.dot(ob_s[...], wb_ref[:, cs], preferred_element_type=F32)
        mix_s[:, cs] = (ga_s[r, :, cs].astype(F32) * ya + gb_s[r, :, cs].astype(F32) * yb).astype(BF16)

    def out_block(j):
        cs = slice(j * PROJ_BLK, (j + 1) * PROJ_BLK)
        o_ref[0, :, cs] = jnp.dot(mix_s[...], wout_ref[:, cs], preferred_element_type=F32)

    P = functools.partial
    n_merge = D_MODEL // PROJ_BLK
    gap = None
    stages = [hgrn_stages(c) for c in range(n_chunks)]
    pools = [P(pool_group, gi) for gi in range(B_GROUPS)] + [pool_carry]
    seq_a = [stages[0][0]]
    for c in range(n_chunks):
        _, small_matmuls, assemble, intra_matmuls, finish = stages[c]
        seq_a += [small_matmuls, gap]
        if c + 1 < n_chunks:
            seq_a.append(stages[c + 1][0])
        seq_a += [assemble, intra_matmuls, gap, finish, gap]
        if pools:
            seq_a.append(pools.pop(0))
    seq_a += pools
    heavy_a = [P(proj_block, k) for k in range(N_PROJ_BLK)]
    seq_b = [x for c in range(n_chunks) for x in (P(decay_chunk, c), gap)]
    heavy_b = [P(merge_block, j) for j in range(n_merge)] + [P(out_block, j) for j in range(n_merge)]
    norm_in()
    for seq, heavy in ((seq_a, heavy_a), (seq_b, heavy_b)):
        n_gaps = seq.count(gap)
        seen = 0
        for item in seq:
            if item is gap:
                for k in range(seen * len(heavy) // n_gaps, (seen + 1) * len(heavy) // n_gaps):
                    heavy[k]()
                seen += 1
            else:
                item()

    if emit_carry:
        st_out_ref[...] = st_s[...]
        pin_out_ref[...] = halo_s[...]


def _ffn_kernel(tm, emit_carry, x_ref, d_ref, halo0_ref, g2_ref, wup_ref, cw_ref, cb_ref, wdn_ref, gf_ref, *rest):
    if emit_carry:
        o_ref, halo_out_ref = rest[:2]
        rest = rest[2:]
    else:
        o_ref = rest[0]
        rest = rest[1:]
    z_s, u_s, halo_s, g_s, acc_s = rest
    t_idx = pl.program_id(1)

    @pl.when(t_idx == 0)
    def _():
        halo_s[...] = halo0_ref[...]

    z_s[...] = _rmsnorm(x_ref[0] + d_ref[0], g2_ref[...]).astype(BF16)

    def up(j):
        u = jnp.dot(z_s[...], wup_ref[:, j * FF_BLK:(j + 1) * FF_BLK], preferred_element_type=F32)
        v = jnp.dot(z_s[...], wup_ref[:, D_FF + j * FF_BLK:D_FF + (j + 1) * FF_BLK],
                    preferred_element_type=F32)
        return u, v

    for j in range(N_FF_BLK):
        cs = slice(j * FF_BLK, (j + 1) * FF_BLK)
        u, v = up(j)
        u_s[0:CONV_HALO, :] = halo_s[j]
        u_s[CONV_HALO:, :] = u
        halo_s[j] = u_s[tm:tm + CONV_HALO, :]
        uc = cb_ref[:, cs] + cw_ref[CONV_W - 1:CONV_W, cs] * u
        for k in range(CONV_W - 1):
            lag = CONV_W - 1 - k
            uc = uc + cw_ref[k:k + 1, cs] * u_s[CONV_HALO - lag:CONV_HALO - lag + tm, :]
        g_s[:, cs] = (jax.nn.silu(uc) * v).astype(BF16)
    for n in range(D_MODEL // FF_BLK):
        ns = slice(n * FF_BLK, (n + 1) * FF_BLK)
        acc_s[:, ns] = (x_ref[0, :, ns] + d_ref[0, :, ns]
                        + jnp.dot(g_s[...], wdn_ref[:, ns], preferred_element_type=F32))
    o_ref[0] = _rmsnorm(acc_s[...], gf_ref[...])
    if emit_carry:
        halo_out_ref[...] = halo_s[...]


def _mixer_call(x, st0, pin0, consts, tm, pos0, emit_carry):
    nb, t_len, d = x.shape
    n_t = t_len // tm
    n_tiles = nb * n_t

    def tile_map(shift):
        def index_map(g):
            tile = jnp.clip(g - shift, 0, n_tiles - 1)
            return (tile // n_t, tile % n_t, 0)
        return index_map

    def const_spec(shape):
        nd = len(shape)
        return pl.BlockSpec(shape, lambda g: (0,) * nd, pipeline_mode=pl.Buffered(1))

    ins = [x, st0, pin0] + consts
    out_shape = [jax.ShapeDtypeStruct((nb, t_len, d), F32)]
    out_specs = [pl.BlockSpec((1, tm, d), tile_map(1))]
    if emit_carry:
        out_shape += [jax.ShapeDtypeStruct(st0.shape, F32), jax.ShapeDtypeStruct(pin0.shape, F32)]
        out_specs += [pl.BlockSpec(st0.shape, lambda g: (0, 0, 0)), pl.BlockSpec(pin0.shape, lambda g: (0, 0))]
    slot = lambda cols, dtype: pltpu.VMEM((2, tm, cols), dtype)
    return pl.pallas_call(
        functools.partial(_mixer_kernel, tm, n_t, pos0, emit_carry),
        grid=(n_tiles + 1,),
        in_specs=[pl.BlockSpec((1, tm, d), tile_map(0))] + [const_spec(a.shape) for a in ins[1:]],
        out_specs=out_specs,
        out_shape=out_shape,
        scratch_shapes=[
            pltpu.VMEM((tm, D_MODEL), BF16),
            pltpu.VMEM((tm, D_MODEL), BF16),
            pltpu.VMEM((tm, A_WIDTH), BF16),
            pltpu.VMEM((tm, B_WIDTH), BF16),
            pltpu.VMEM((A_HEADS, A_VDIM, A_KDIM), F32),
            pltpu.VMEM((POOL_HALO, B_WIDTH), F32),
            slot(A_FDIM, F32),
            slot(A_FDIM, F32),
            slot(A_FDIM, F32),
            slot(A_FDIM, BF16),
            slot(A_FDIM, BF16),
            slot(A_WIDTH, BF16),
            slot(A_WIDTH, BF16),
            slot(B_WIDTH, F32),
            slot(D_MODEL, BF16),
            slot(D_MODEL, BF16),
        ],
        compiler_params=pltpu.CompilerParams(dimension_semantics=("arbitrary",),
                                             vmem_limit_bytes=VMEM_LIMIT_BYTES),
        name="mixer_meta" if emit_carry else "mixer",
    )(*ins)


def _ffn_call(x, delta, halo0, consts, tm, emit_carry):
    nb, t_len, d = x.shape
    tile_spec = pl.BlockSpec((1, tm, d), lambda b, t: (b, t, 0))

    def const_spec(shape):
        nd = len(shape)
        return pl.BlockSpec(shape, lambda b, t: (0,) * nd, pipeline_mode=pl.Buffered(1))

    ins = [x, delta, halo0] + consts
    out_shape = [jax.ShapeDtypeStruct((nb, t_len, d), F32)]
    out_specs = [tile_spec]
    if emit_carry:
        out_shape += [jax.ShapeDtypeStruct(halo0.shape, F32)]
        out_specs += [pl.BlockSpec(halo0.shape, lambda b, t: (0, 0, 0))]
    return pl.pallas_call(
        functools.partial(_ffn_kernel, tm, emit_carry),
        grid=(nb, t_len // tm),
        in_specs=[tile_spec, tile_spec] + [const_spec(a.shape) for a in ins[2:]],
        out_specs=out_specs,
        out_shape=out_shape,
        scratch_shapes=[
            pltpu.VMEM((tm, D_MODEL), BF16),
            pltpu.VMEM((CONV_HALO + tm, FF_BLK), F32),
            pltpu.VMEM((N_FF_BLK, CONV_HALO, FF_BLK), F32),
            pltpu.VMEM((tm, D_FF), BF16),
            pltpu.VMEM((tm, D_MODEL), F32),
        ],
        compiler_params=pltpu.CompilerParams(dimension_semantics=("arbitrary", "arbitrary"),
                                             vmem_limit_bytes=VMEM_LIMIT_BYTES),
        name="ffn_meta" if emit_carry else "ffn",
    )(*ins)


def kernel(x, meta_tokens, lb_logits, norm1_g, w_in, b_f, head_norm_g, w_pool, pool_scale, w_branch_a,
           w_branch_b, w_out, norm2_g, w_up, conv_w, conv_b, w_down, final_norm_g):
    B, S, D = x.shape
    assert D == D_MODEL and S % MAIN_TM == 0 and meta_tokens.shape[0] == N_META

    row = lambda p: p.reshape(1, -1).astype(F32)
    mixer_consts = [
        lb_logits.astype(F32), row(norm1_g[0]), w_in[0].astype(BF16), row(b_f[0]), row(head_norm_g[0]),
        w_pool[0].astype(BF16), row(pool_scale[0]), w_branch_a[0].astype(BF16), w_branch_b[0].astype(BF16),
        w_out[0].astype(BF16),
    ]
    ffn_consts = [
        row(norm2_g[0]), w_up[0].astype(BF16), conv_w[0].astype(F32), row(conv_b[0]),
        w_down[0].astype(BF16), row(final_norm_g),
    ]

    xm = jnp.concatenate([jnp.zeros((PAD, D), x.dtype), meta_tokens.astype(x.dtype)], axis=0)[None]
    st0 = jnp.zeros((A_HEADS, A_VDIM, A_KDIM), F32)
    pin0 = jnp.zeros((POOL_HALO, B_WIDTH), F32)
    halo0 = jnp.zeros((N_FF_BLK, CONV_HALO, FF_BLK), F32)
    dm, st_m, pin_m = _mixer_call(xm, st0, pin0, mixer_consts, CHUNK, 1 - PAD, True)
    _, halo_m = _ffn_call(xm, dm, halo0, ffn_consts, CHUNK, True)

    (delta,) = _mixer_call(x, st_m, pin_m, mixer_consts, MAIN_TM, N_META + 1, False)
    (out,) = _ffn_call(x, delta, halo_m, ffn_consts, MAIN_TM, False)
    return out
```

```python
import functools

import jax
import jax.numpy as jnp
from jax import lax
from jax.experimental import pallas as pl
from jax.experimental.pallas import tpu as pltpu

D_MODEL = 1024
CHUNK = 64
N_META = 16
PAD = CHUNK - N_META
A_KDIM = 128
A_HEADS = D_MODEL // A_KDIM
A_FDIM = A_HEADS * A_KDIM
A_WIDTH = D_MODEL // 2
A_VDIM = A_WIDTH // A_HEADS
POOL_WINDOWS = (2, 4, 8, 16)
B_GROUPS = 4
B_WIDTH = D_MODEL // 2
B_GDIM = B_WIDTH // B_GROUPS
D_FF = ((8 * D_MODEL // 3 + 127) // 128) * 128
CONV_W = 3
EPS = 1e-6

SUB = 8
N_SUB = CHUNK // SUB
BF16_ROWS = 16
POOL_HALO = 16
CONV_HALO = 8
MAIN_TM = 8 * CHUNK
FFN_TM = 2 * MAIN_TM
PROJ_BLK = 256
N_PROJ_BLK = (2 * A_FDIM + 2 * A_WIDTH + B_WIDTH + 2 * D_MODEL) // PROJ_BLK
FF_BLK = 256
N_FF_BLK = D_FF // FF_BLK
VMEM_LIMIT_BYTES = 60000 * 1024

F32 = jnp.float32
BF16 = jnp.bfloat16


def _rmsnorm(x, g):
    ms = jnp.mean(x * x, axis=-1, keepdims=True)
    return x * lax.rsqrt(ms + EPS) * g


def _mixer_kernel(tm, n_t, pos0, emit_carry,
                  x_ref, st0_ref, pin0_ref, lbl_ref, g1_ref, win_ref, bf_ref, hg_ref, wpool_ref, pscale_ref,
                  wa_ref, wb_ref, wout_ref, *rest):
    if emit_carry:
        o_ref, st_out_ref, pin_out_ref = rest[:3]
        rest = rest[3:]
    else:
        o_ref = rest[0]
        rest = rest[1:]
    (z_s, mix_s, oa_s, ob_s, st_s, halo_s, q_s, a_s, b_s, qa_s, kd_s, v_s, og_s, pin_s, ga_s, gb_s) = rest
    slotted = (q_s, a_s, b_s, qa_s, kd_s, v_s, og_s, pin_s, ga_s, gb_s)
    n_chunks = tm // CHUNK
    g = pl.program_id(0)
    p = g % 2
    r = 1 - p
    tr = jnp.maximum(g - 1, 0) % n_t

    @pl.when(g == 0)
    def _():
        for s in slotted:
            s[1] = jnp.zeros(s.shape[1:], s.dtype)

    @pl.when(tr == 0)
    def _():
        st_s[...] = st0_ref[...]
        halo_s[...] = pin0_ref[...]

    def norm_in():
        z_s[...] = _rmsnorm(x_ref[0], g1_ref[...]).astype(BF16)

    def proj(col0):
        return jnp.dot(z_s[...], win_ref[:, col0:col0 + PROJ_BLK], preferred_element_type=F32)

    def proj_block(k):
        col0 = k * PROJ_BLK
        if col0 < A_FDIM:
            q_s[p, :, col0:col0 + PROJ_BLK] = jax.nn.silu(proj(col0))
        elif col0 < 2 * A_FDIM:
            c = col0 - A_FDIM
            l0 = lbl_ref[0:1, c:c + PROJ_BLK]
            l1 = lbl_ref[1:2, c:c + PROJ_BLK]
            lmax = jnp.maximum(l0, l1)
            e0 = jnp.exp(l0 - lmax)
            e1 = jnp.exp(l1 - lmax)
            lb = e0 / (e0 + e1)
            sig = jax.nn.sigmoid(proj(col0) + bf_ref[:, c:c + PROJ_BLK])
            a_s[p, :, c:c + PROJ_BLK] = jnp.log2(lb + (1.0 - lb) * sig)
            b_s[p, :, c:c + PROJ_BLK] = jnp.log2((1.0 - lb) * (1.0 - sig))
        elif col0 < 2 * A_FDIM + A_WIDTH:
            c = col0 - 2 * A_FDIM
            v_s[p, :, c:c + PROJ_BLK] = proj(col0).astype(BF16)
        elif col0 < 2 * A_FDIM + 2 * A_WIDTH:
            c = col0 - (2 * A_FDIM + A_WIDTH)
            og_s[p, :, c:c + PROJ_BLK] = jax.nn.silu(proj(col0)).astype(BF16)
        elif col0 < 2 * A_FDIM + 2 * A_WIDTH + B_WIDTH:
            c = col0 - (2 * A_FDIM + 2 * A_WIDTH)
            pin_s[p, :, c:c + PROJ_BLK] = proj(col0)
        elif col0 < 2 * A_FDIM + 2 * A_WIDTH + B_WIDTH + D_MODEL:
            c = col0 - (2 * A_FDIM + 2 * A_WIDTH + B_WIDTH)
            ga_s[p, :, c:c + PROJ_BLK] = jax.nn.sigmoid(proj(col0)).astype(BF16)
        else:
            c = col0 - (2 * A_FDIM + 2 * A_WIDTH + B_WIDTH + D_MODEL)
            gb_s[p, :, c:c + PROJ_BLK] = jax.nn.sigmoid(proj(col0)).astype(BF16)

    row_i = lax.broadcasted_iota(jnp.int32, (CHUNK, 2 * CHUNK), 0)
    col_i = lax.broadcasted_iota(jnp.int32, (CHUNK, 2 * CHUNK), 1)
    tril2 = ((col_i % CHUNK) <= row_i).astype(BF16)

    def decay_chunk(c):
        rows = slice(c * CHUNK, (c + 1) * CHUNK)
        lf = a_s[p, rows, :]
        hi = lf.astype(BF16)
        lo = (lf - hi.astype(F32)).astype(BF16)
        a_c = jnp.dot(tril2, jnp.concatenate([hi, lo], axis=0), preferred_element_type=F32)
        b_c = a_c - b_s[p, rows, :]
        a_s[p, rows, :] = a_c
        b_s[p, rows, :] = b_c
        a_last = a_c[CHUNK - 1:CHUNK, :]
        qa_s[p, rows, :] = (q_s[p, rows, :] * jnp.exp2(a_c)).astype(BF16)
        kd_s[p, rows, :] = jnp.exp2(a_last - b_c).astype(BF16)

    lane_i = lax.broadcasted_iota(jnp.int32, (SUB, CHUNK), 1)
    subl_i = lax.broadcasted_iota(jnp.int32, (SUB, CHUNK), 0)

    def hgrn_stages(c):
        r0 = c * CHUNK
        rows = slice(r0, r0 + CHUNK)
        kls = [slice(h * A_KDIM, (h + 1) * A_KDIM) for h in range(A_HEADS)]
        vls = [slice(h * A_VDIM, (h + 1) * A_VDIM) for h in range(A_HEADS)]
        diag, qs, kp, o_inter, off, upd, scores, intra = ({} for _ in range(8))

        def operands():
            for h in range(A_HEADS):
                kl = kls[h]
                q = q_s[r, rows, kl]
                a = a_s[r, rows, kl]
                b = b_s[r, rows, kl]
                for i in range(N_SUB):
                    rs = slice(i * SUB, (i + 1) * SUB)
                    q_i = q[rs]
                    a_i = a[rs]
                    acc = jnp.zeros((SUB, CHUNK), F32)
                    for s in range(SUB):
                        rk = r0 + i * SUB + s
                        b_row = b_s[r, rk:rk + 1, kl]
                        col = jnp.sum(q_i * jnp.exp2(a_i - b_row), axis=-1, keepdims=True)
                        acc = jnp.where(lane_i == i * SUB + s, col, acc)
                    diag[h, i] = jnp.where(lane_i <= i * SUB + subl_i, acc, 0.0)
                    if i > 0:
                        rb = r0 + i * SUB
                        anchor = a_s[r, rb - 1:rb, kl]
                        qs[h, i] = (q_i * jnp.exp2(a_i - anchor)).astype(BF16)
                        kpi = jnp.exp2(anchor - b[0:i * SUB])
                        kpi = jnp.concatenate([kpi, jnp.zeros((CHUNK - i * SUB, A_KDIM), F32)], axis=0)
                        kp[h, i] = kpi.astype(BF16)

        def small_matmuls():
            for h in range(A_HEADS):
                kl = kls[h]
                o_inter[h] = pl.dot(qa_s[r, rows, kl], st_s[h].astype(BF16), trans_b=True)
                for i in range(1, N_SUB):
                    off[h, i] = pl.dot(qs[h, i], kp[h, i], trans_b=True)
                upd[h] = pl.dot(v_s[r, rows, vls[h]], kd_s[r, rows, kl], trans_a=True)

        def assemble():
            for h in range(A_HEADS):
                blocks = [diag[h, 0]] + [diag[h, i] + off[h, i] for i in range(1, N_SUB)]
                scores[h] = jnp.concatenate(blocks, axis=0).astype(BF16)
                a_last = a_s[r, r0 + CHUNK - 1:r0 + CHUNK, kls[h]]
                st_s[h] = st_s[h] * jnp.exp2(a_last) + upd[h]

        def intra_matmuls():
            for h in range(A_HEADS):
                intra[h] = jnp.dot(scores[h], v_s[r, rows, vls[h]], preferred_element_type=F32)

        def finish():
            for h in range(A_HEADS):
                vl = vls[h]
                o = o_inter[h] + intra[h]
                ms = jnp.mean(o * o, axis=-1, keepdims=True)
                o = o * lax.rsqrt(ms + EPS) * hg_ref[:, vl] * og_s[r, rows, vl].astype(F32)
                oa_s[rows, vl] = o.astype(BF16)

        return [operands, small_matmuls, assemble, intra_matmuls, finish]

    def pool_group(gi):
        w = POOL_WINDOWS[gi]
        row_e = lax.broadcasted_iota(jnp.int32, (tm, B_GDIM), 0)
        pos = (tr * tm + row_e + pos0).astype(F32)
        if True:
            gl = slice(gi * B_GDIM, (gi + 1) * B_GDIM)
            cur = pin_s[r, :, gl]
            xs = jnp.concatenate([halo_s[:, gl], cur], axis=0)
            acc = xs
            span = 1
            while span < w:
                acc = acc + pltpu.roll(acc, span, 0)
                span *= 2
            cnt = jnp.clip(pos, 1.0, float(w))
            d = acc[POOL_HALO:] / cnt - cur
            y = jnp.dot(d.astype(BF16), wpool_ref[gi], preferred_element_type=F32)
            ob_s[:, gl] = (y * pscale_ref[:, gl]).astype(BF16)

    def pool_carry():
        halo_s[...] = pin_s[r, tm - POOL_HALO:tm, :]

    def merge_block(j):
        cs = slice(j * PROJ_BLK, (j + 1) * PROJ_BLK)
        ya = jnp.dot(oa_s[...], wa_ref[:, cs], preferred_element_type=F32)
        yb = jnp.dot(ob_s[...], wb_ref[:, cs], preferred_element_type=F32)
        mix_s[:, cs] = (ga_s[r, :, cs].astype(F32) * ya + gb_s[r, :, cs].astype(F32) * yb).astype(BF16)

    def out_block(j):
        cs = slice(j * PROJ_BLK, (j + 1) * PROJ_BLK)
        o_ref[0, :, cs] = jnp.dot(mix_s[...], wout_ref[:, cs], preferred_element_type=F32)

    P = functools.partial
    n_merge = D_MODEL // PROJ_BLK
    gap = None
    stages = [hgrn_stages(c) for c in range(n_chunks)]
    pools = [P(pool_group, gi) for gi in range(B_GROUPS)] + [pool_carry]
    seq_a = [stages[0][0]]
    for c in range(n_chunks):
        _, small_matmuls, assemble, intra_matmuls, finish = stages[c]
        seq_a += [small_matmuls, gap]
        if c + 1 < n_chunks:
            seq_a.append(stages[c + 1][0])
        seq_a += [assemble, intra_matmuls, gap, finish, gap]
        if pools:
            seq_a.append(pools.pop(0))
    seq_a += pools
    heavy_a = [P(proj_block, k) for k in range(N_PROJ_BLK)]
    seq_b = [x for c in range(n_chunks) for x in (P(decay_chunk, c), gap)]
    heavy_b = [P(merge_block, j) for j in range(n_merge)] + [P(out_block, j) for j in range(n_merge)]
    norm_in()
    for seq, heavy in ((seq_a, heavy_a), (seq_b, heavy_b)):
        n_gaps = seq.count(gap)
        seen = 0
        for item in seq:
            if item is gap:
                for k in range(seen * len(heavy) // n_gaps, (seen + 1) * len(heavy) // n_gaps):
                    heavy[k]()
                seen += 1
            else:
                item()

    if emit_carry:
        st_out_ref[...] = st_s[...]
        pin_out_ref[...] = halo_s[...]


def _ffn_kernel(tm, emit_carry, x_ref, d_ref, halo0_ref, g2_ref, wup_ref, cw_ref, cb_ref, wdn_ref, gf_ref, *rest):
    if emit_carry:
        o_ref, halo_out_ref = rest[:2]
        rest = rest[2:]
    else:
        o_ref = rest[0]
        rest = rest[1:]
    z_s, u_s, halo_s, g_s, acc_s = rest
    t_idx = pl.program_id(1)

    @pl.when(t_idx == 0)
    def _():
        halo_s[...] = halo0_ref[...]

    z_s[...] = _rmsnorm(x_ref[0] + d_ref[0], g2_ref[...]).astype(BF16)

    def up(j):
        u = jnp.dot(z_s[...], wup_ref[:, j * FF_BLK:(j + 1) * FF_BLK], preferred_element_type=F32)
        v = jnp.dot(z_s[...], wup_ref[:, D_FF + j * FF_BLK:D_FF + (j + 1) * FF_BLK],
                    preferred_element_type=F32)
        return u, v

    for j in range(N_FF_BLK):
        cs = slice(j * FF_BLK, (j + 1) * FF_BLK)
        u, v = up(j)
        u_s[0:CONV_HALO, :] = halo_s[j]
        u_s[CONV_HALO:, :] = u
        halo_s[j] = u_s[tm:tm + CONV_HALO, :]
        uc = cb_ref[:, cs] + cw_ref[CONV_W - 1:CONV_W, cs] * u
        for k in range(CONV_W - 1):
            lag = CONV_W - 1 - k
            uc = uc + cw_ref[k:k + 1, cs] * u_s[CONV_HALO - lag:CONV_HALO - lag + tm, :]
        g_s[:, cs] = (jax.nn.silu(uc) * v).astype(BF16)
    for n in range(D_MODEL // FF_BLK):
        ns = slice(n * FF_BLK, (n + 1) * FF_BLK)
        acc_s[:, ns] = (x_ref[0, :, ns] + d_ref[0, :, ns]
                        + jnp.dot(g_s[...], wdn_ref[:, ns], preferred_element_type=F32))
    o_ref[0] = _rmsnorm(acc_s[...], gf_ref[...])
    if emit_carry:
        halo_out_ref[...] = halo_s[...]


def _mixer_call(x, st0, pin0, consts, tm, pos0, emit_carry):
    nb, t_len, d = x.shape
    n_t = t_len // tm
    n_tiles = nb * n_t

    def tile_map(shift):
        def index_map(g):
            tile = jnp.clip(g - shift, 0, n_tiles - 1)
            return (tile // n_t, tile % n_t, 0)
        return index_map

    def const_spec(shape):
        nd = len(shape)
        return pl.BlockSpec(shape, lambda g: (0,) * nd, pipeline_mode=pl.Buffered(1))

    ins = [x, st0, pin0] + consts
    out_shape = [jax.ShapeDtypeStruct((nb, t_len, d), F32)]
    out_specs = [pl.BlockSpec((1, tm, d), tile_map(1))]
    if emit_carry:
        out_shape += [jax.ShapeDtypeStruct(st0.shape, F32), jax.ShapeDtypeStruct(pin0.shape, F32)]
        out_specs += [pl.BlockSpec(st0.shape, lambda g: (0, 0, 0)), pl.BlockSpec(pin0.shape, lambda g: (0, 0))]
    slot = lambda cols, dtype: pltpu.VMEM((2, tm, cols), dtype)
    return pl.pallas_call(
        functools.partial(_mixer_kernel, tm, n_t, pos0, emit_carry),
        grid=(n_tiles + 1,),
        in_specs=[pl.BlockSpec((1, tm, d), tile_map(0))] + [const_spec(a.shape) for a in ins[1:]],
        out_specs=out_specs,
        out_shape=out_shape,
        scratch_shapes=[
            pltpu.VMEM((tm, D_MODEL), BF16),
            pltpu.VMEM((tm, D_MODEL), BF16),
            pltpu.VMEM((tm, A_WIDTH), BF16),
            pltpu.VMEM((tm, B_WIDTH), BF16),
            pltpu.VMEM((A_HEADS, A_VDIM, A_KDIM), F32),
            pltpu.VMEM((POOL_HALO, B_WIDTH), F32),
            slot(A_FDIM, F32),
            slot(A_FDIM, F32),
            slot(A_FDIM, F32),
            slot(A_FDIM, BF16),
            slot(A_FDIM, BF16),
            slot(A_WIDTH, BF16),
            slot(A_WIDTH, BF16),
            slot(B_WIDTH, F32),
            slot(D_MODEL, BF16),
            slot(D_MODEL, BF16),
        ],
        compiler_params=pltpu.CompilerParams(dimension_semantics=("arbitrary",),
                                             vmem_limit_bytes=VMEM_LIMIT_BYTES),
        name="mixer_meta" if emit_carry else "mixer",
    )(*ins)


def _ffn_call(x, delta, halo0, consts, tm, emit_carry):
    nb, t_len, d = x.shape
    tile_spec = pl.BlockSpec((1, tm, d), lambda b, t: (b, t, 0))

    def const_spec(shape):
        nd = len(shape)
        return pl.BlockSpec(shape, lambda b, t: (0,) * nd, pipeline_mode=pl.Buffered(1))

    ins = [x, delta, halo0] + consts
    out_shape = [jax.ShapeDtypeStruct((nb, t_len, d), F32)]
    out_specs = [tile_spec]
    if emit_carry:
        out_shape += [jax.ShapeDtypeStruct(halo0.shape, F32)]
        out_specs += [pl.BlockSpec(halo0.shape, lambda b, t: (0, 0, 0))]
    return pl.pallas_call(
        functools.partial(_ffn_kernel, tm, emit_carry),
        grid=(nb, t_len // tm),
        in_specs=[tile_spec, tile_spec] + [const_spec(a.shape) for a in ins[2:]],
        out_specs=out_specs,
        out_shape=out_shape,
        scratch_shapes=[
            pltpu.VMEM((tm, D_MODEL), BF16),
            pltpu.VMEM((CONV_HALO + tm, FF_BLK), F32),
            pltpu.VMEM((N_FF_BLK, CONV_HALO, FF_BLK), F32),
            pltpu.VMEM((tm, D_FF), BF16),
            pltpu.VMEM((tm, D_MODEL), F32),
        ],
        compiler_params=pltpu.CompilerParams(dimension_semantics=("arbitrary", "arbitrary"),
                                             vmem_limit_bytes=VMEM_LIMIT_BYTES),
        name="ffn_meta" if emit_carry else "ffn",
    )(*ins)


def kernel(x, meta_tokens, lb_logits, norm1_g, w_in, b_f, head_norm_g, w_pool, pool_scale, w_branch_a,
           w_branch_b, w_out, norm2_g, w_up, conv_w, conv_b, w_down, final_norm_g):
    B, S, D = x.shape
    assert D == D_MODEL and S % FFN_TM == 0 and meta_tokens.shape[0] == N_META

    row = lambda p: p.reshape(1, -1).astype(F32)
    mixer_consts = [
        lb_logits.astype(F32), row(norm1_g[0]), w_in[0].astype(BF16), row(b_f[0]), row(head_norm_g[0]),
        w_pool[0].astype(BF16), row(pool_scale[0]), w_branch_a[0].astype(BF16), w_branch_b[0].astype(BF16),
        w_out[0].astype(BF16),
    ]
    ffn_consts = [
        row(norm2_g[0]), w_up[0].astype(BF16), conv_w[0].astype(F32), row(conv_b[0]),
        w_down[0].astype(BF16), row(final_norm_g),
    ]

    xm = jnp.concatenate([jnp.zeros((PAD, D), x.dtype), meta_tokens.astype(x.dtype)], axis=0)[None]
    st0 = jnp.zeros((A_HEADS, A_VDIM, A_KDIM), F32)
    pin0 = jnp.zeros((POOL_HALO, B_WIDTH), F32)
    halo0 = jnp.zeros((N_FF_BLK, CONV_HALO, FF_BLK), F32)
    dm, st_m, pin_m = _mixer_call(xm, st0, pin0, mixer_consts, CHUNK, 1 - PAD, True)
    _, halo_m = _ffn_call(xm, dm, halo0, ffn_consts, CHUNK, True)

    (delta,) = _mixer_call(x, st_m, pin_m, mixer_consts, MAIN_TM, N_META + 1, False)
    (out,) = _ffn_call(x, delta, halo_m, ffn_consts, FFN_TM, False)
    return out
```

```python
import functools

import jax
import jax.numpy as jnp
from jax import lax
from jax.experimental import pallas as pl
from jax.experimental.pallas import tpu as pltpu

D_MODEL = 1024
CHUNK = 64
N_META = 16
PAD = CHUNK - N_META
A_KDIM = 128
A_HEADS = D_MODEL // A_KDIM
A_FDIM = A_HEADS * A_KDIM
A_WIDTH = D_MODEL // 2
A_VDIM = A_WIDTH // A_HEADS
POOL_WINDOWS = (2, 4, 8, 16)
B_GROUPS = 4
B_WIDTH = D_MODEL // 2
B_GDIM = B_WIDTH // B_GROUPS
D_FF = ((8 * D_MODEL // 3 + 127) // 128) * 128
CONV_W = 3
EPS = 1e-6

SUB = 8
N_SUB = CHUNK // SUB
BF16_ROWS = 16
POOL_HALO = 16
CONV_HALO = 8
MAIN_TM = 8 * CHUNK
FFN_TM = 2 * MAIN_TM
PROJ_BLK = 256
N_PROJ_BLK = (2 * A_FDIM + 2 * A_WIDTH + B_WIDTH + 2 * D_MODEL) // PROJ_BLK
FF_BLK = 256
N_FF_BLK = D_FF // FF_BLK
VMEM_LIMIT_BYTES = 60000 * 1024

F32 = jnp.float32
BF16 = jnp.bfloat16


def _rmsnorm(x, g):
    ms = jnp.mean(x * x, axis=-1, keepdims=True)
    return x * lax.rsqrt(ms + EPS) * g


def _mixer_kernel(tm, n_t, pos0, emit_carry,
                  x_ref, st0_ref, pin0_ref, lbl_ref, g1_ref, win_ref, bf_ref, hg_ref, wpool_ref, pscale_ref,
                  wa_ref, wb_ref, wout_ref, *rest):
    if emit_carry:
        o_ref, st_out_ref, pin_out_ref = rest[:3]
        rest = rest[3:]
    else:
        o_ref = rest[0]
        rest = rest[1:]
    (z_s, mix_s, oa_s, ob_s, st_s, halo_s, sc_s, q_s, a_s, b_s, qa_s, kd_s, v_s, og_s, pin_s, ga_s, gb_s) = rest
    slotted = (q_s, a_s, b_s, qa_s, kd_s, v_s, og_s, pin_s, ga_s, gb_s)
    n_chunks = tm // CHUNK
    g = pl.program_id(0)
    p = g % 2
    r = 1 - p
    tr = jnp.maximum(g - 1, 0) % n_t

    @pl.when(g == 0)
    def _():
        for s in slotted:
            s[1] = jnp.zeros(s.shape[1:], s.dtype)

    @pl.when(tr == 0)
    def _():
        st_s[...] = st0_ref[...]
        halo_s[...] = pin0_ref[...]

    def norm_in():
        z_s[...] = _rmsnorm(x_ref[0], g1_ref[...]).astype(BF16)

    def proj(col0):
        return jnp.dot(z_s[...], win_ref[:, col0:col0 + PROJ_BLK], preferred_element_type=F32)

    def proj_block(k):
        col0 = k * PROJ_BLK
        if col0 < A_FDIM:
            q_s[p, :, col0:col0 + PROJ_BLK] = jax.nn.silu(proj(col0))
        elif col0 < 2 * A_FDIM:
            c = col0 - A_FDIM
            l0 = lbl_ref[0:1, c:c + PROJ_BLK]
            l1 = lbl_ref[1:2, c:c + PROJ_BLK]
            lmax = jnp.maximum(l0, l1)
            e0 = jnp.exp(l0 - lmax)
            e1 = jnp.exp(l1 - lmax)
            lb = e0 / (e0 + e1)
            sig = jax.nn.sigmoid(proj(col0) + bf_ref[:, c:c + PROJ_BLK])
            a_s[p, :, c:c + PROJ_BLK] = jnp.log2(lb + (1.0 - lb) * sig)
            b_s[p, :, c:c + PROJ_BLK] = jnp.log2((1.0 - lb) * (1.0 - sig))
        elif col0 < 2 * A_FDIM + A_WIDTH:
            c = col0 - 2 * A_FDIM
            v_s[p, :, c:c + PROJ_BLK] = proj(col0).astype(BF16)
        elif col0 < 2 * A_FDIM + 2 * A_WIDTH:
            c = col0 - (2 * A_FDIM + A_WIDTH)
            og_s[p, :, c:c + PROJ_BLK] = jax.nn.silu(proj(col0)).astype(BF16)
        elif col0 < 2 * A_FDIM + 2 * A_WIDTH + B_WIDTH:
            c = col0 - (2 * A_FDIM + 2 * A_WIDTH)
            pin_s[p, :, c:c + PROJ_BLK] = proj(col0)
        elif col0 < 2 * A_FDIM + 2 * A_WIDTH + B_WIDTH + D_MODEL:
            c = col0 - (2 * A_FDIM + 2 * A_WIDTH + B_WIDTH)
            ga_s[p, :, c:c + PROJ_BLK] = jax.nn.sigmoid(proj(col0)).astype(BF16)
        else:
            c = col0 - (2 * A_FDIM + 2 * A_WIDTH + B_WIDTH + D_MODEL)
            gb_s[p, :, c:c + PROJ_BLK] = jax.nn.sigmoid(proj(col0)).astype(BF16)

    row_i = lax.broadcasted_iota(jnp.int32, (CHUNK, 2 * CHUNK), 0)
    col_i = lax.broadcasted_iota(jnp.int32, (CHUNK, 2 * CHUNK), 1)
    tril2 = ((col_i % CHUNK) <= row_i).astype(BF16)

    def decay_chunk(c):
        rows = slice(c * CHUNK, (c + 1) * CHUNK)
        lf = a_s[p, rows, :]
        hi = lf.astype(BF16)
        lo = (lf - hi.astype(F32)).astype(BF16)
        a_c = jnp.dot(tril2, jnp.concatenate([hi, lo], axis=0), preferred_element_type=F32)
        b_c = a_c - b_s[p, rows, :]
        a_s[p, rows, :] = a_c
        b_s[p, rows, :] = b_c
        a_last = a_c[CHUNK - 1:CHUNK, :]
        qa_s[p, rows, :] = (q_s[p, rows, :] * jnp.exp2(a_c)).astype(BF16)
        kd_s[p, rows, :] = jnp.exp2(a_last - b_c).astype(BF16)

    row_c = lax.broadcasted_iota(jnp.int32, (CHUNK, CHUNK), 0)
    col_c = lax.broadcasted_iota(jnp.int32, (CHUNK, CHUNK), 1)
    diag_mask = (col_c <= row_c) & (col_c >= (row_c // SUB) * SUB)

    def hgrn_stages(c):
        r0 = c * CHUNK
        rows = slice(r0, r0 + CHUNK)
        kls = [slice(h * A_KDIM, (h + 1) * A_KDIM) for h in range(A_HEADS)]
        vls = [slice(h * A_VDIM, (h + 1) * A_VDIM) for h in range(A_HEADS)]
        qs, kp, o_inter, off, upd, scores, intra = ({} for _ in range(7))

        def operands():
            for h in range(A_HEADS):
                kl = kls[h]
                q = q_s[r, rows, kl]
                a = a_s[r, rows, kl]
                b = b_s[r, rows, kl]
                for i in range(N_SUB):
                    rs = slice(i * SUB, (i + 1) * SUB)
                    q_i = q[rs]
                    a_i = a[rs]
                    sc_s[c % 2, h, rs, :] = jnp.zeros((SUB, CHUNK), F32)
                    for s in range(SUB):
                        rk = r0 + i * SUB + s
                        b_row = b_s[r, rk:rk + 1, kl]
                        col = jnp.sum(q_i * jnp.exp2(a_i - b_row), axis=-1, keepdims=True)
                        sc_s[c % 2, h, rs, i * SUB + s:i * SUB + s + 1] = col
                    if i > 0:
                        rb = r0 + i * SUB
                        anchor = a_s[r, rb - 1:rb, kl]
                        qs[h, i] = (q_i * jnp.exp2(a_i - anchor)).astype(BF16)
                        kpi = jnp.exp2(anchor - b[0:i * SUB])
                        kpi = jnp.concatenate([kpi, jnp.zeros((CHUNK - i * SUB, A_KDIM), F32)], axis=0)
                        kp[h, i] = kpi.astype(BF16)

        def small_matmuls():
            for h in range(A_HEADS):
                kl = kls[h]
                o_inter[h] = pl.dot(qa_s[r, rows, kl], st_s[h].astype(BF16), trans_b=True)
                for i in range(1, N_SUB):
                    off[h, i] = pl.dot(qs[h, i], kp[h, i], trans_b=True)
                upd[h] = pl.dot(v_s[r, rows, vls[h]], kd_s[r, rows, kl], trans_a=True)

        def assemble():
            for h in range(A_HEADS):
                offs = jnp.concatenate([jnp.zeros((SUB, CHUNK), F32)] + [off[h, i] for i in range(1, N_SUB)], axis=0)
                scores[h] = (jnp.where(diag_mask, sc_s[c % 2, h], 0.0) + offs).astype(BF16)
                a_last = a_s[r, r0 + CHUNK - 1:r0 + CHUNK, kls[h]]
                st_s[h] = st_s[h] * jnp.exp2(a_last) + upd[h]

        def intra_matmuls():
            for h in range(A_HEADS):
                intra[h] = jnp.dot(scores[h], v_s[r, rows, vls[h]], preferred_element_type=F32)

        def finish():
            for h in range(A_HEADS):
                vl = vls[h]
                o = o_inter[h] + intra[h]
                ms = jnp.mean(o * o, axis=-1, keepdims=True)
                o = o * lax.rsqrt(ms + EPS) * hg_ref[:, vl] * og_s[r, rows, vl].astype(F32)
                oa_s[rows, vl] = o.astype(BF16)

        return [operands, small_matmuls, assemble, intra_matmuls, finish]

    def pool_group(gi):
        w = POOL_WINDOWS[gi]
        row_e = lax.broadcasted_iota(jnp.int32, (tm, B_GDIM), 0)
        pos = (tr * tm + row_e + pos0).astype(F32)
        if True:
            gl = slice(gi * B_GDIM, (gi + 1) * B_GDIM)
            cur = pin_s[r, :, gl]
            xs = jnp.concatenate([halo_s[:, gl], cur], axis=0)
            acc = xs
            span = 1
            while span < w:
                acc = acc + pltpu.roll(acc, span, 0)
                span *= 2
            cnt = jnp.clip(pos, 1.0, float(w))
            d = acc[POOL_HALO:] / cnt - cur
            y = jnp.dot(d.astype(BF16), wpool_ref[gi], preferred_element_type=F32)
            ob_s[:, gl] = (y * pscale_ref[:, gl]).astype(BF16)

    def pool_carry():
        halo_s[...] = pin_s[r, tm - POOL_HALO:tm, :]

    def merge_block(j):
        cs = slice(j * PROJ_BLK, (j + 1) * PROJ_BLK)
        ya = jnp.dot(oa_s[...], wa_ref[:, cs], preferred_element_type=F32)
        yb = jnp.dot(ob_s[...], wb_ref[:, cs], preferred_element_type=F32)
        mix_s[:, cs] = (ga_s[r, :, cs].astype(F32) * ya + gb_s[r, :, cs].astype(F32) * yb).astype(BF16)

    def out_block(j):
        cs = slice(j * PROJ_BLK, (j + 1) * PROJ_BLK)
        o_ref[0, :, cs] = jnp.dot(mix_s[...], wout_ref[:, cs], preferred_element_type=F32)

    P = functools.partial
    n_merge = D_MODEL // PROJ_BLK
    gap = None
    stages = [hgrn_stages(c) for c in range(n_chunks)]
    pools = [P(pool_group, gi) for gi in range(B_GROUPS)] + [pool_carry]
    seq_a = [stages[0][0]]
    for c in range(n_chunks):
        _, small_matmuls, assemble, intra_matmuls, finish = stages[c]
        seq_a += [small_matmuls, gap]
        if c + 1 < n_chunks:
            seq_a.append(stages[c + 1][0])
        seq_a += [assemble, intra_matmuls, gap, finish, gap]
        if pools:
            seq_a.append(pools.pop(0))
    seq_a += pools
    heavy_a = [P(proj_block, k) for k in range(N_PROJ_BLK)]
    seq_b = [x for c in range(n_chunks) for x in (P(decay_chunk, c), gap)]
    heavy_b = [P(merge_block, j) for j in range(n_merge)] + [P(out_block, j) for j in range(n_merge)]
    norm_in()
    for seq, heavy in ((seq_a, heavy_a), (seq_b, heavy_b)):
        n_gaps = seq.count(gap)
        seen = 0
        for item in seq:
            if item is gap:
                for k in range(seen * len(heavy) // n_gaps, (seen + 1) * len(heavy) // n_gaps):
                    heavy[k]()
                seen += 1
            else:
                item()

    if emit_carry:
        st_out_ref[...] = st_s[...]
        pin_out_ref[...] = halo_s[...]


def _ffn_kernel(tm, emit_carry, x_ref, d_ref, halo0_ref, g2_ref, wup_ref, cw_ref, cb_ref, wdn_ref, gf_ref, *rest):
    if emit_carry:
        o_ref, halo_out_ref = rest[:2]
        rest = rest[2:]
    else:
        o_ref = rest[0]
        rest = rest[1:]
    z_s, u_s, halo_s, g_s, acc_s = rest
    t_idx = pl.program_id(1)

    @pl.when(t_idx == 0)
    def _():
        halo_s[...] = halo0_ref[...]

    z_s[...] = _rmsnorm(x_ref[0] + d_ref[0], g2_ref[...]).astype(BF16)

    def up(j):
        u = jnp.dot(z_s[...], wup_ref[:, j * FF_BLK:(j + 1) * FF_BLK], preferred_element_type=F32)
        v = jnp.dot(z_s[...], wup_ref[:, D_FF + j * FF_BLK:D_FF + (j + 1) * FF_BLK],
                    preferred_element_type=F32)
        return u, v

    for j in range(N_FF_BLK):
        cs = slice(j * FF_BLK, (j + 1) * FF_BLK)
        u, v = up(j)
        u_s[0:CONV_HALO, :] = halo_s[j]
        u_s[CONV_HALO:, :] = u
        halo_s[j] = u_s[tm:tm + CONV_HALO, :]
        uc = cb_ref[:, cs] + cw_ref[CONV_W - 1:CONV_W, cs] * u
        for k in range(CONV_W - 1):
            lag = CONV_W - 1 - k
            uc = uc + cw_ref[k:k + 1, cs] * u_s[CONV_HALO - lag:CONV_HALO - lag + tm, :]
        g_s[:, cs] = (jax.nn.silu(uc) * v).astype(BF16)
    for n in range(D_MODEL // FF_BLK):
        ns = slice(n * FF_BLK, (n + 1) * FF_BLK)
        acc_s[:, ns] = (x_ref[0, :, ns] + d_ref[0, :, ns]
                        + jnp.dot(g_s[...], wdn_ref[:, ns], preferred_element_type=F32))
    o_ref[0] = _rmsnorm(acc_s[...], gf_ref[...])
    if emit_carry:
        halo_out_ref[...] = halo_s[...]


def _mixer_call(x, st0, pin0, consts, tm, pos0, emit_carry):
    nb, t_len, d = x.shape
    n_t = t_len // tm
    n_tiles = nb * n_t

    def tile_map(shift):
        def index_map(g):
            tile = jnp.clip(g - shift, 0, n_tiles - 1)
            return (tile // n_t, tile % n_t, 0)
        return index_map

    def const_spec(shape):
        nd = len(shape)
        return pl.BlockSpec(shape, lambda g: (0,) * nd, pipeline_mode=pl.Buffered(1))

    ins = [x, st0, pin0] + consts
    out_shape = [jax.ShapeDtypeStruct((nb, t_len, d), F32)]
    out_specs = [pl.BlockSpec((1, tm, d), tile_map(1))]
    if emit_carry:
        out_shape += [jax.ShapeDtypeStruct(st0.shape, F32), jax.ShapeDtypeStruct(pin0.shape, F32)]
        out_specs += [pl.BlockSpec(st0.shape, lambda g: (0, 0, 0)), pl.BlockSpec(pin0.shape, lambda g: (0, 0))]
    slot = lambda cols, dtype: pltpu.VMEM((2, tm, cols), dtype)
    return pl.pallas_call(
        functools.partial(_mixer_kernel, tm, n_t, pos0, emit_carry),
        grid=(n_tiles + 1,),
        in_specs=[pl.BlockSpec((1, tm, d), tile_map(0))] + [const_spec(a.shape) for a in ins[1:]],
        out_specs=out_specs,
        out_shape=out_shape,
        scratch_shapes=[
            pltpu.VMEM((tm, D_MODEL), BF16),
            pltpu.VMEM((tm, D_MODEL), BF16),
            pltpu.VMEM((tm, A_WIDTH), BF16),
            pltpu.VMEM((tm, B_WIDTH), BF16),
            pltpu.VMEM((A_HEADS, A_VDIM, A_KDIM), F32),
            pltpu.VMEM((POOL_HALO, B_WIDTH), F32),
            pltpu.VMEM((2, A_HEADS, CHUNK, CHUNK), F32),
            slot(A_FDIM, F32),
            slot(A_FDIM, F32),
            slot(A_FDIM, F32),
            slot(A_FDIM, BF16),
            slot(A_FDIM, BF16),
            slot(A_WIDTH, BF16),
            slot(A_WIDTH, BF16),
            slot(B_WIDTH, F32),
            slot(D_MODEL, BF16),
            slot(D_MODEL, BF16),
        ],
        compiler_params=pltpu.CompilerParams(dimension_semantics=("arbitrary",),
                                             vmem_limit_bytes=VMEM_LIMIT_BYTES),
        name="mixer_meta" if emit_carry else "mixer",
    )(*ins)


def _ffn_call(x, delta, halo0, consts, tm, emit_carry):
    nb, t_len, d = x.shape
    tile_spec = pl.BlockSpec((1, tm, d), lambda b, t: (b, t, 0))

    def const_spec(shape):
        nd = len(shape)
        return pl.BlockSpec(shape, lambda b, t: (0,) * nd, pipeline_mode=pl.Buffered(1))

    ins = [x, delta, halo0] + consts
    out_shape = [jax.ShapeDtypeStruct((nb, t_len, d), F32)]
    out_specs = [tile_spec]
    if emit_carry:
        out_shape += [jax.ShapeDtypeStruct(halo0.shape, F32)]
        out_specs += [pl.BlockSpec(halo0.shape, lambda b, t: (0, 0, 0))]
    return pl.pallas_call(
        functools.partial(_ffn_kernel, tm, emit_carry),
        grid=(nb, t_len // tm),
        in_specs=[tile_spec, tile_spec] + [const_spec(a.shape) for a in ins[2:]],
        out_specs=out_specs,
        out_shape=out_shape,
        scratch_shapes=[
            pltpu.VMEM((tm, D_MODEL), BF16),
            pltpu.VMEM((CONV_HALO + tm, FF_BLK), F32),
            pltpu.VMEM((N_FF_BLK, CONV_HALO, FF_BLK), F32),
            pltpu.VMEM((tm, D_FF), BF16),
            pltpu.VMEM((tm, D_MODEL), F32),
        ],
        compiler_params=pltpu.CompilerParams(dimension_semantics=("arbitrary", "arbitrary"),
                                             vmem_limit_bytes=VMEM_LIMIT_BYTES),
        name="ffn_meta" if emit_carry else "ffn",
    )(*ins)


def kernel(x, meta_tokens, lb_logits, norm1_g, w_in, b_f, head_norm_g, w_pool, pool_scale, w_branch_a,
           w_branch_b, w_out, norm2_g, w_up, conv_w, conv_b, w_down, final_norm_g):
    B, S, D = x.shape
    assert D == D_MODEL and S % FFN_TM == 0 and meta_tokens.shape[0] == N_META

    row = lambda p: p.reshape(1, -1).astype(F32)
    mixer_consts = [
        lb_logits.astype(F32), row(norm1_g[0]), w_in[0].astype(BF16), row(b_f[0]), row(head_norm_g[0]),
        w_pool[0].astype(BF16), row(pool_scale[0]), w_branch_a[0].astype(BF16), w_branch_b[0].astype(BF16),
        w_out[0].astype(BF16),
    ]
    ffn_consts = [
        row(norm2_g[0]), w_up[0].astype(BF16), conv_w[0].astype(F32), row(conv_b[0]),
        w_down[0].astype(BF16), row(final_norm_g),
    ]

    xm = jnp.concatenate([jnp.zeros((PAD, D), x.dtype), meta_tokens.astype(x.dtype)], axis=0)[None]
    st0 = jnp.zeros((A_HEADS, A_VDIM, A_KDIM), F32)
    pin0 = jnp.zeros((POOL_HALO, B_WIDTH), F32)
    halo0 = jnp.zeros((N_FF_BLK, CONV_HALO, FF_BLK), F32)
    dm, st_m, pin_m = _mixer_call(xm, st0, pin0, mixer_consts, CHUNK, 1 - PAD, True)
    _, halo_m = _ffn_call(xm, dm, halo0, ffn_consts, CHUNK, True)

    (delta,) = _mixer_call(x, st_m, pin_m, mixer_consts, MAIN_TM, N_META + 1, False)
    (out,) = _ffn_call(x, delta, halo_m, ffn_consts, FFN_TM, False)
    return out
```

```python
import functools

import jax
import jax.numpy as jnp
from jax import lax
from jax.experimental import pallas as pl
from jax.experimental.pallas import tpu as pltpu

D_MODEL = 1024
CHUNK = 64
N_META = 16
PAD = CHUNK - N_META
A_KDIM = 128
A_HEADS = D_MODEL // A_KDIM
A_FDIM = A_HEADS * A_KDIM
A_WIDTH = D_MODEL // 2
A_VDIM = A_WIDTH // A_HEADS
POOL_WINDOWS = (2, 4, 8, 16)
B_GROUPS = 4
B_WIDTH = D_MODEL // 2
B_GDIM = B_WIDTH // B_GROUPS
D_FF = ((8 * D_MODEL // 3 + 127) // 128) * 128
CONV_W = 3
EPS = 1e-6

SUB = 8
N_SUB = CHUNK // SUB
POOL_HALO = 16
CONV_HALO = 8
MAIN_TM = 8 * CHUNK
FFN_TM = 2 * MAIN_TM
PROJ_BLK = 256
N_PROJ_BLK = (2 * A_FDIM + 2 * A_WIDTH + B_WIDTH + 2 * D_MODEL) // PROJ_BLK
FF_BLK = 256
N_FF_BLK = D_FF // FF_BLK
VMEM_LIMIT_BYTES = 60000 * 1024

F32 = jnp.float32
BF16 = jnp.bfloat16


def _rmsnorm(x, g):
    ms = jnp.mean(x * x, axis=-1, keepdims=True)
    return x * lax.rsqrt(ms + EPS) * g


def _mixer_kernel(tm, n_t, pos0, emit_carry,
                  x_ref, st0_ref, pin0_ref, lbl_ref, g1_ref, win_ref, bf_ref, hg_ref, wpool_ref, pscale_ref,
                  wa_ref, wb_ref, wout_ref, *rest):
    if emit_carry:
        o_ref, st_out_ref, pin_out_ref = rest[:3]
        rest = rest[3:]
    else:
        o_ref = rest[0]
        rest = rest[1:]
    (z_s, mix_s, oa_s, ob_s, st_s, halo_s, sc_s, q_s, a_s, b_s, qa_s, kd_s, v_s, og_s, pin_s, ga_s, gb_s) = rest
    slotted = (q_s, a_s, b_s, qa_s, kd_s, v_s, og_s, pin_s, ga_s, gb_s)
    n_chunks = tm // CHUNK
    g = pl.program_id(0)
    p = g % 2
    r = 1 - p
    tr = jnp.maximum(g - 1, 0) % n_t

    @pl.when(g == 0)
    def _():
        for s in slotted:
            s[1] = jnp.zeros(s.shape[1:], s.dtype)

    @pl.when(tr == 0)
    def _():
        st_s[...] = st0_ref[...]
        halo_s[...] = pin0_ref[...]

    def norm_in():
        z_s[...] = _rmsnorm(x_ref[0], g1_ref[...]).astype(BF16)

    def proj(col0):
        return jnp.dot(z_s[...], win_ref[:, col0:col0 + PROJ_BLK], preferred_element_type=F32)

    def proj_block(k):
        col0 = k * PROJ_BLK
        if col0 < A_FDIM:
            q_s[p, :, col0:col0 + PROJ_BLK] = jax.nn.silu(proj(col0))
        elif col0 < 2 * A_FDIM:
            c = col0 - A_FDIM
            l0 = lbl_ref[0:1, c:c + PROJ_BLK]
            l1 = lbl_ref[1:2, c:c + PROJ_BLK]
            lmax = jnp.maximum(l0, l1)
            e0 = jnp.exp(l0 - lmax)
            e1 = jnp.exp(l1 - lmax)
            lb = e0 / (e0 + e1)
            sig = jax.nn.sigmoid(proj(col0) + bf_ref[:, c:c + PROJ_BLK])
            a_s[p, :, c:c + PROJ_BLK] = jnp.log2(lb + (1.0 - lb) * sig)
            b_s[p, :, c:c + PROJ_BLK] = jnp.log2((1.0 - lb) * (1.0 - sig))
        elif col0 < 2 * A_FDIM + A_WIDTH:
            c = col0 - 2 * A_FDIM
            v_s[p, :, c:c + PROJ_BLK] = proj(col0).astype(BF16)
        elif col0 < 2 * A_FDIM + 2 * A_WIDTH:
            c = col0 - (2 * A_FDIM + A_WIDTH)
            og_s[p, :, c:c + PROJ_BLK] = jax.nn.silu(proj(col0)).astype(BF16)
        elif col0 < 2 * A_FDIM + 2 * A_WIDTH + B_WIDTH:
            c = col0 - (2 * A_FDIM + 2 * A_WIDTH)
            pin_s[p, :, c:c + PROJ_BLK] = proj(col0)
        elif col0 < 2 * A_FDIM + 2 * A_WIDTH + B_WIDTH + D_MODEL:
            c = col0 - (2 * A_FDIM + 2 * A_WIDTH + B_WIDTH)
            ga_s[p, :, c:c + PROJ_BLK] = jax.nn.sigmoid(proj(col0)).astype(BF16)
        else:
            c = col0 - (2 * A_FDIM + 2 * A_WIDTH + B_WIDTH + D_MODEL)
            gb_s[p, :, c:c + PROJ_BLK] = jax.nn.sigmoid(proj(col0)).astype(BF16)

    row_i = lax.broadcasted_iota(jnp.int32, (CHUNK, 2 * CHUNK), 0)
    col_i = lax.broadcasted_iota(jnp.int32, (CHUNK, 2 * CHUNK), 1)
    tril2 = ((col_i % CHUNK) <= row_i).astype(BF16)

    def decay_chunk(c):
        rows = slice(c * CHUNK, (c + 1) * CHUNK)
        lf = a_s[p, rows, :]
        hi = lf.astype(BF16)
        lo = (lf - hi.astype(F32)).astype(BF16)
        a_c = jnp.dot(tril2, jnp.concatenate([hi, lo], axis=0), preferred_element_type=F32)
        b_c = a_c - b_s[p, rows, :]
        a_s[p, rows, :] = a_c
        b_s[p, rows, :] = b_c
        a_last = a_c[CHUNK - 1:CHUNK, :]
        qa_s[p, rows, :] = (q_s[p, rows, :] * jnp.exp2(a_c)).astype(BF16)
        kd_s[p, rows, :] = jnp.exp2(a_last - b_c).astype(BF16)

    row_c = lax.broadcasted_iota(jnp.int32, (CHUNK, CHUNK), 0)
    col_c = lax.broadcasted_iota(jnp.int32, (CHUNK, CHUNK), 1)
    diag_mask = (col_c <= row_c) & (col_c >= (row_c // SUB) * SUB)

    def hgrn_stages(c):
        r0 = c * CHUNK
        rows = slice(r0, r0 + CHUNK)
        kls = [slice(h * A_KDIM, (h + 1) * A_KDIM) for h in range(A_HEADS)]
        vls = [slice(h * A_VDIM, (h + 1) * A_VDIM) for h in range(A_HEADS)]
        qs, kp, o_inter, off, upd, scores, intra = ({} for _ in range(7))

        def operands():
            for h in range(A_HEADS):
                kl = kls[h]
                q = q_s[r, rows, kl]
                a = a_s[r, rows, kl]
                b = b_s[r, rows, kl]
                for i in range(N_SUB):
                    rs = slice(i * SUB, (i + 1) * SUB)
                    q_i = q[rs]
                    a_i = a[rs]
                    sc_s[c % 2, h, rs, :] = jnp.zeros((SUB, CHUNK), F32)
                    for s in range(SUB):
                        rk = r0 + i * SUB + s
                        b_row = b_s[r, rk:rk + 1, kl]
                        col = jnp.sum(q_i * jnp.exp2(a_i - b_row), axis=-1, keepdims=True)
                        sc_s[c % 2, h, rs, i * SUB + s:i * SUB + s + 1] = col
                    if i > 0:
                        rb = r0 + i * SUB
                        anchor = a_s[r, rb - 1:rb, kl]
                        qs[h, i] = (q_i * jnp.exp2(a_i - anchor)).astype(BF16)
                        kpi = jnp.exp2(anchor - b[0:i * SUB])
                        kpi = jnp.concatenate([kpi, jnp.zeros((CHUNK - i * SUB, A_KDIM), F32)], axis=0)
                        kp[h, i] = kpi.astype(BF16)

        def small_matmuls():
            for h in range(A_HEADS):
                kl = kls[h]
                o_inter[h] = pl.dot(qa_s[r, rows, kl], st_s[h].astype(BF16), trans_b=True)
                for i in range(1, N_SUB):
                    off[h, i] = pl.dot(qs[h, i], kp[h, i], trans_b=True)
                upd[h] = pl.dot(v_s[r, rows, vls[h]], kd_s[r, rows, kl], trans_a=True)

        def assemble():
            for h in range(A_HEADS):
                offs = jnp.concatenate([jnp.zeros((SUB, CHUNK), F32)] + [off[h, i] for i in range(1, N_SUB)], axis=0)
                scores[h] = (jnp.where(diag_mask, sc_s[c % 2, h], 0.0) + offs).astype(BF16)
                a_last = a_s[r, r0 + CHUNK - 1:r0 + CHUNK, kls[h]]
                st_s[h] = st_s[h] * jnp.exp2(a_last) + upd[h]

        def intra_matmuls():
            for h in range(A_HEADS):
                intra[h] = jnp.dot(scores[h], v_s[r, rows, vls[h]], preferred_element_type=F32)

        def finish():
            for h in range(A_HEADS):
                vl = vls[h]
                o = o_inter[h] + intra[h]
                ms = jnp.mean(o * o, axis=-1, keepdims=True)
                o = o * lax.rsqrt(ms + EPS) * hg_ref[:, vl] * og_s[r, rows, vl].astype(F32)
                oa_s[rows, vl] = o.astype(BF16)

        return [operands, small_matmuls, assemble, intra_matmuls, finish]

    def pool_group(gi):
        w = POOL_WINDOWS[gi]
        row_e = lax.broadcasted_iota(jnp.int32, (tm, B_GDIM), 0)
        pos = (tr * tm + row_e + pos0).astype(F32)
        gl = slice(gi * B_GDIM, (gi + 1) * B_GDIM)
        cur = pin_s[r, :, gl]
        xs = jnp.concatenate([halo_s[:, gl], cur], axis=0)
        acc = xs
        span = 1
        while span < w:
            acc = acc + pltpu.roll(acc, span, 0)
            span *= 2
        cnt = jnp.clip(pos, 1.0, float(w))
        d = acc[POOL_HALO:] / cnt - cur
        y = jnp.dot(d.astype(BF16), wpool_ref[gi], preferred_element_type=F32)
        ob_s[:, gl] = (y * pscale_ref[:, gl]).astype(BF16)

    def pool_carry():
        halo_s[...] = pin_s[r, tm - POOL_HALO:tm, :]

    def merge_block(j):
        cs = slice(j * PROJ_BLK, (j + 1) * PROJ_BLK)
        ya = jnp.dot(oa_s[...], wa_ref[:, cs], preferred_element_type=F32)
        yb = jnp.dot(ob_s[...], wb_ref[:, cs], preferred_element_type=F32)
        mix_s[:, cs] = (ga_s[r, :, cs].astype(F32) * ya + gb_s[r, :, cs].astype(F32) * yb).astype(BF16)

    def out_block(j):
        cs = slice(j * PROJ_BLK, (j + 1) * PROJ_BLK)
        o_ref[0, :, cs] = jnp.dot(mix_s[...], wout_ref[:, cs], preferred_element_type=F32)

    P = functools.partial
    n_merge = D_MODEL // PROJ_BLK
    gap = None
    stages = [hgrn_stages(c) for c in range(n_chunks)]
    pools = [P(pool_group, gi) for gi in range(B_GROUPS)] + [pool_carry]
    seq_a = [stages[0][0]]
    for c in range(n_chunks):
        _, small_matmuls, assemble, intra_matmuls, finish = stages[c]
        seq_a += [small_matmuls, gap]
        if c + 1 < n_chunks:
            seq_a.append(stages[c + 1][0])
        seq_a += [assemble, intra_matmuls, gap, finish, gap]
        if pools:
            seq_a.append(pools.pop(0))
    seq_a += pools
    heavy_a = [P(proj_block, k) for k in range(N_PROJ_BLK)]
    seq_b = [x for c in range(n_chunks) for x in (P(decay_chunk, c), gap)]
    heavy_b = [P(merge_block, j) for j in range(n_merge)] + [P(out_block, j) for j in range(n_merge)]
    norm_in()
    for seq, heavy in ((seq_a, heavy_a), (seq_b, heavy_b)):
        n_gaps = seq.count(gap)
        seen = 0
        for item in seq:
            if item is gap:
                for k in range(seen * len(heavy) // n_gaps, (seen + 1) * len(heavy) // n_gaps):
                    heavy[k]()
                seen += 1
            else:
                item()

    if emit_carry:
        st_out_ref[...] = st_s[...]
        pin_out_ref[...] = halo_s[...]


def _ffn_kernel(tm, emit_carry, x_ref, d_ref, halo0_ref, g2_ref, wup_ref, cw_ref, cb_ref, wdn_ref, gf_ref, *rest):
    if emit_carry:
        o_ref, halo_out_ref = rest[:2]
        rest = rest[2:]
    else:
        o_ref = rest[0]
        rest = rest[1:]
    z_s, u_s, halo_s, g_s, acc_s = rest
    t_idx = pl.program_id(1)

    @pl.when(t_idx == 0)
    def _():
        halo_s[...] = halo0_ref[...]

    z_s[...] = _rmsnorm(x_ref[0] + d_ref[0], g2_ref[...]).astype(BF16)

    def up(j):
        u = jnp.dot(z_s[...], wup_ref[:, j * FF_BLK:(j + 1) * FF_BLK], preferred_element_type=F32)
        v = jnp.dot(z_s[...], wup_ref[:, D_FF + j * FF_BLK:D_FF + (j + 1) * FF_BLK],
                    preferred_element_type=F32)
        return u, v

    for j in range(N_FF_BLK):
        cs = slice(j * FF_BLK, (j + 1) * FF_BLK)
        u, v = up(j)
        u_s[0:CONV_HALO, :] = halo_s[j]
        u_s[CONV_HALO:, :] = u
        halo_s[j] = u_s[tm:tm + CONV_HALO, :]
        uc = cb_ref[:, cs] + cw_ref[CONV_W - 1:CONV_W, cs] * u
        for k in range(CONV_W - 1):
            lag = CONV_W - 1 - k
            uc = uc + cw_ref[k:k + 1, cs] * u_s[CONV_HALO - lag:CONV_HALO - lag + tm, :]
        g_s[:, cs] = (jax.nn.silu(uc) * v).astype(BF16)
    for n in range(D_MODEL // FF_BLK):
        ns = slice(n * FF_BLK, (n + 1) * FF_BLK)
        acc_s[:, ns] = (x_ref[0, :, ns] + d_ref[0, :, ns]
                        + jnp.dot(g_s[...], wdn_ref[:, ns], preferred_element_type=F32))
    o_ref[0] = _rmsnorm(acc_s[...], gf_ref[...])
    if emit_carry:
        halo_out_ref[...] = halo_s[...]


def _mixer_call(x, st0, pin0, consts, tm, pos0, emit_carry):
    nb, t_len, d = x.shape
    n_t = t_len // tm
    n_tiles = nb * n_t

    def tile_map(shift):
        def index_map(g):
            tile = jnp.clip(g - shift, 0, n_tiles - 1)
            return (tile // n_t, tile % n_t, 0)
        return index_map

    def const_spec(shape):
        nd = len(shape)
        return pl.BlockSpec(shape, lambda g: (0,) * nd, pipeline_mode=pl.Buffered(1))

    ins = [x, st0, pin0] + consts
    out_shape = [jax.ShapeDtypeStruct((nb, t_len, d), F32)]
    out_specs = [pl.BlockSpec((1, tm, d), tile_map(1))]
    if emit_carry:
        out_shape += [jax.ShapeDtypeStruct(st0.shape, F32), jax.ShapeDtypeStruct(pin0.shape, F32)]
        out_specs += [pl.BlockSpec(st0.shape, lambda g: (0, 0, 0)), pl.BlockSpec(pin0.shape, lambda g: (0, 0))]
    slot = lambda cols, dtype: pltpu.VMEM((2, tm, cols), dtype)
    return pl.pallas_call(
        functools.partial(_mixer_kernel, tm, n_t, pos0, emit_carry),
        grid=(n_tiles + 1,),
        in_specs=[pl.BlockSpec((1, tm, d), tile_map(0))] + [const_spec(a.shape) for a in ins[1:]],
        out_specs=out_specs,
        out_shape=out_shape,
        scratch_shapes=[
            pltpu.VMEM((tm, D_MODEL), BF16),
            pltpu.VMEM((tm, D_MODEL), BF16),
            pltpu.VMEM((tm, A_WIDTH), BF16),
            pltpu.VMEM((tm, B_WIDTH), BF16),
            pltpu.VMEM((A_HEADS, A_VDIM, A_KDIM), F32),
            pltpu.VMEM((POOL_HALO, B_WIDTH), F32),
            pltpu.VMEM((2, A_HEADS, CHUNK, CHUNK), F32),
            slot(A_FDIM, F32),
            slot(A_FDIM, F32),
            slot(A_FDIM, F32),
            slot(A_FDIM, BF16),
            slot(A_FDIM, BF16),
            slot(A_WIDTH, BF16),
            slot(A_WIDTH, BF16),
            slot(B_WIDTH, F32),
            slot(D_MODEL, BF16),
            slot(D_MODEL, BF16),
        ],
        compiler_params=pltpu.CompilerParams(dimension_semantics=("arbitrary",),
                                             vmem_limit_bytes=VMEM_LIMIT_BYTES),
        name="mixer_meta" if emit_carry else "mixer",
    )(*ins)


def _ffn_call(x, delta, halo0, consts, tm, emit_carry):
    nb, t_len, d = x.shape
    tile_spec = pl.BlockSpec((1, tm, d), lambda b, t: (b, t, 0))

    def const_spec(shape):
        nd = len(shape)
        return pl.BlockSpec(shape, lambda b, t: (0,) * nd, pipeline_mode=pl.Buffered(1))

    ins = [x, delta, halo0] + consts
    out_shape = [jax.ShapeDtypeStruct((nb, t_len, d), F32)]
    out_specs = [tile_spec]
    if emit_carry:
        out_shape += [jax.ShapeDtypeStruct(halo0.shape, F32)]
        out_specs += [pl.BlockSpec(halo0.shape, lambda b, t: (0, 0, 0))]
    return pl.pallas_call(
        functools.partial(_ffn_kernel, tm, emit_carry),
        grid=(nb, t_len // tm),
        in_specs=[tile_spec, tile_spec] + [const_spec(a.shape) for a in ins[2:]],
        out_specs=out_specs,
        out_shape=out_shape,
        scratch_shapes=[
            pltpu.VMEM((tm, D_MODEL), BF16),
            pltpu.VMEM((CONV_HALO + tm, FF_BLK), F32),
            pltpu.VMEM((N_FF_BLK, CONV_HALO, FF_BLK), F32),
            pltpu.VMEM((tm, D_FF), BF16),
            pltpu.VMEM((tm, D_MODEL), F32),
        ],
        compiler_params=pltpu.CompilerParams(dimension_semantics=("arbitrary", "arbitrary"),
                                             vmem_limit_bytes=VMEM_LIMIT_BYTES),
        name="ffn_meta" if emit_carry else "ffn",
    )(*ins)


def kernel(x, meta_tokens, lb_logits, norm1_g, w_in, b_f, head_norm_g, w_pool, pool_scale, w_branch_a,
           w_branch_b, w_out, norm2_g, w_up, conv_w, conv_b, w_down, final_norm_g):
    B, S, D = x.shape
    assert D == D_MODEL and S % FFN_TM == 0 and meta_tokens.shape[0] == N_META

    row = lambda p: p.reshape(1, -1).astype(F32)
    mixer_consts = [
        lb_logits.astype(F32), row(norm1_g[0]), w_in[0].astype(BF16), row(b_f[0]), row(head_norm_g[0]),
        w_pool[0].astype(BF16), row(pool_scale[0]), w_branch_a[0].astype(BF16), w_branch_b[0].astype(BF16),
        w_out[0].astype(BF16),
    ]
    ffn_consts = [
        row(norm2_g[0]), w_up[0].astype(BF16), conv_w[0].astype(F32), row(conv_b[0]),
        w_down[0].astype(BF16), row(final_norm_g),
    ]

    xm = jnp.concatenate([jnp.zeros((PAD, D), x.dtype), meta_tokens.astype(x.dtype)], axis=0)[None]
    st0 = jnp.zeros((A_HEADS, A_VDIM, A_KDIM), F32)
    pin0 = jnp.zeros((POOL_HALO, B_WIDTH), F32)
    halo0 = jnp.zeros((N_FF_BLK, CONV_HALO, FF_BLK), F32)
    dm, st_m, pin_m = _mixer_call(xm, st0, pin0, mixer_consts, CHUNK, 1 - PAD, True)
    _, halo_m = _ffn_call(xm, dm, halo0, ffn_consts, CHUNK, True)

    (delta,) = _mixer_call(x, st_m, pin_m, mixer_consts, MAIN_TM, N_META + 1, False)
    (out,) = _ffn_call(x, delta, halo_m, ffn_consts, FFN_TM, False)
    return out
```

```python
import functools

import jax
import jax.numpy as jnp
from jax import lax
from jax.experimental import pallas as pl
from jax.experimental.pallas import tpu as pltpu

D_MODEL = 1024
CHUNK = 64
N_META = 16
PAD = CHUNK - N_META
A_KDIM = 128
A_HEADS = D_MODEL // A_KDIM
A_FDIM = A_HEADS * A_KDIM
A_WIDTH = D_MODEL // 2
A_VDIM = A_WIDTH // A_HEADS
POOL_WINDOWS = (2, 4, 8, 16)
B_GROUPS = 4
B_WIDTH = D_MODEL // 2
B_GDIM = B_WIDTH // B_GROUPS
D_FF = ((8 * D_MODEL // 3 + 127) // 128) * 128
CONV_W = 3
EPS = 1e-6

SUB = 8
N_SUB = CHUNK // SUB
BF16_ROWS = 16
POOL_HALO = 16
CONV_HALO = 8
MAIN_TM = 8 * CHUNK
FFN_TM = 2 * MAIN_TM
PROJ_BLK = 256
N_PROJ_BLK = (2 * A_FDIM + 2 * A_WIDTH + B_WIDTH + 2 * D_MODEL) // PROJ_BLK
FF_BLK = 256
N_FF_BLK = D_FF // FF_BLK
VMEM_LIMIT_BYTES = 60000 * 1024

F32 = jnp.float32
BF16 = jnp.bfloat16


def _rmsnorm(x, g):
    ms = jnp.mean(x * x, axis=-1, keepdims=True)
    return x * lax.rsqrt(ms + EPS) * g


def _mixer_kernel(tm, n_t, pos0, emit_carry,
                  x_ref, st0_ref, pin0_ref, lbl_ref, g1_ref, win_ref, bf_ref, hg_ref, wpool_ref, pscale_ref,
                  wa_ref, wb_ref, wout_ref, *rest):
    if emit_carry:
        o_ref, st_out_ref, pin_out_ref = rest[:3]
        rest = rest[3:]
    else:
        o_ref = rest[0]
        rest = rest[1:]
    (z_s, mix_s, oa_s, ob_s, st_s, halo_s, sc_s, oraw_s, q_s, a_s, b_s, qa_s, kd_s, v_s, og_s, pin_s, ga_s, gb_s) = rest
    slotted = (q_s, a_s, b_s, qa_s, kd_s, v_s, og_s, pin_s, ga_s, gb_s)
    n_chunks = tm // CHUNK
    g = pl.program_id(0)
    p = g % 2
    r = 1 - p
    tr = jnp.maximum(g - 1, 0) % n_t

    @pl.when(g == 0)
    def _():
        for s in slotted:
            s[1] = jnp.zeros(s.shape[1:], s.dtype)

    @pl.when(tr == 0)
    def _():
        st_s[...] = st0_ref[...]
        halo_s[...] = pin0_ref[...]

    def norm_in():
        z_s[...] = _rmsnorm(x_ref[0], g1_ref[...]).astype(BF16)

    def proj(col0):
        return jnp.dot(z_s[...], win_ref[:, col0:col0 + PROJ_BLK], preferred_element_type=F32)

    def proj_block(k):
        col0 = k * PROJ_BLK
        if col0 < A_FDIM:
            q_s[p, :, col0:col0 + PROJ_BLK] = jax.nn.silu(proj(col0))
        elif col0 < 2 * A_FDIM:
            c = col0 - A_FDIM
            l0 = lbl_ref[0:1, c:c + PROJ_BLK]
            l1 = lbl_ref[1:2, c:c + PROJ_BLK]
            lmax = jnp.maximum(l0, l1)
            e0 = jnp.exp(l0 - lmax)
            e1 = jnp.exp(l1 - lmax)
            lb = e0 / (e0 + e1)
            sig = jax.nn.sigmoid(proj(col0) + bf_ref[:, c:c + PROJ_BLK])
            a_s[p, :, c:c + PROJ_BLK] = jnp.log2(lb + (1.0 - lb) * sig)
            b_s[p, :, c:c + PROJ_BLK] = jnp.log2((1.0 - lb) * (1.0 - sig))
        elif col0 < 2 * A_FDIM + A_WIDTH:
            c = col0 - 2 * A_FDIM
            v_s[p, :, c:c + PROJ_BLK] = proj(col0).astype(BF16)
        elif col0 < 2 * A_FDIM + 2 * A_WIDTH:
            c = col0 - (2 * A_FDIM + A_WIDTH)
            og_s[p, :, c:c + PROJ_BLK] = jax.nn.silu(proj(col0)).astype(BF16)
        elif col0 < 2 * A_FDIM + 2 * A_WIDTH + B_WIDTH:
            c = col0 - (2 * A_FDIM + 2 * A_WIDTH)
            pin_s[p, :, c:c + PROJ_BLK] = proj(col0)
        elif col0 < 2 * A_FDIM + 2 * A_WIDTH + B_WIDTH + D_MODEL:
            c = col0 - (2 * A_FDIM + 2 * A_WIDTH + B_WIDTH)
            ga_s[p, :, c:c + PROJ_BLK] = jax.nn.sigmoid(proj(col0)).astype(BF16)
        else:
            c = col0 - (2 * A_FDIM + 2 * A_WIDTH + B_WIDTH + D_MODEL)
            gb_s[p, :, c:c + PROJ_BLK] = jax.nn.sigmoid(proj(col0)).astype(BF16)

    row_i = lax.broadcasted_iota(jnp.int32, (CHUNK, 2 * CHUNK), 0)
    col_i = lax.broadcasted_iota(jnp.int32, (CHUNK, 2 * CHUNK), 1)
    tril2 = ((col_i % CHUNK) <= row_i).astype(BF16)

    def decay_chunk(c):
        rows = slice(c * CHUNK, (c + 1) * CHUNK)
        lf = a_s[p, rows, :]
        hi = lf.astype(BF16)
        lo = (lf - hi.astype(F32)).astype(BF16)
        a_c = jnp.dot(tril2, jnp.concatenate([hi, lo], axis=0), preferred_element_type=F32)
        b_c = a_c - b_s[p, rows, :]
        a_s[p, rows, :] = a_c
        b_s[p, rows, :] = b_c
        a_last = a_c[CHUNK - 1:CHUNK, :]
        qa_s[p, rows, :] = (q_s[p, rows, :] * jnp.exp2(a_c)).astype(BF16)
        kd_s[p, rows, :] = jnp.exp2(a_last - b_c).astype(BF16)

    row_c = lax.broadcasted_iota(jnp.int32, (CHUNK, CHUNK), 0)
    col_c = lax.broadcasted_iota(jnp.int32, (CHUNK, CHUNK), 1)
    diag_mask = (col_c <= row_c) & (col_c >= (row_c // SUB) * SUB)

    def hgrn_stages(c):
        r0 = c * CHUNK
        rows = slice(r0, r0 + CHUNK)
        kls = [slice(h * A_KDIM, (h + 1) * A_KDIM) for h in range(A_HEADS)]
        vls = [slice(h * A_VDIM, (h + 1) * A_VDIM) for h in range(A_HEADS)]
        qs, kp, o_inter, off, upd, scores, intra = ({} for _ in range(7))

        def operands():
            for h in range(A_HEADS):
                kl = kls[h]
                q = q_s[r, rows, kl]
                a = a_s[r, rows, kl]
                b = b_s[r, rows, kl]
                for i in range(N_SUB):
                    rs = slice(i * SUB, (i + 1) * SUB)
                    q_i = q[rs]
                    a_i = a[rs]
                    sc_s[c % 2, h, rs, :] = jnp.zeros((SUB, CHUNK), F32)
                    for s in range(SUB):
                        rk = r0 + i * SUB + s
                        b_row = b_s[r, rk:rk + 1, kl]
                        col = jnp.sum(q_i * jnp.exp2(a_i - b_row), axis=-1, keepdims=True)
                        sc_s[c % 2, h, rs, i * SUB + s:i * SUB + s + 1] = col
                    if i > 0:
                        rb = r0 + i * SUB
                        anchor = a_s[r, rb - 1:rb, kl]
                        qs[h, i] = (q_i * jnp.exp2(a_i - anchor)).astype(BF16)
                        kpi = jnp.exp2(anchor - b[0:i * SUB])
                        kpi = jnp.concatenate([kpi, jnp.zeros((CHUNK - i * SUB, A_KDIM), F32)], axis=0)
                        kp[h, i] = kpi.astype(BF16)

        def small_matmuls():
            for h in range(A_HEADS):
                kl = kls[h]
                o_inter[h] = pl.dot(qa_s[r, rows, kl], st_s[h].astype(BF16), trans_b=True)
                for i in range(1, N_SUB):
                    off[h, i] = pl.dot(qs[h, i], kp[h, i], trans_b=True)
                upd[h] = pl.dot(v_s[r, rows, vls[h]], kd_s[r, rows, kl], trans_a=True)

        def assemble():
            for h in range(A_HEADS):
                offs = jnp.concatenate([jnp.zeros((SUB, CHUNK), F32)] + [off[h, i] for i in range(1, N_SUB)], axis=0)
                scores[h] = (jnp.where(diag_mask, sc_s[c % 2, h], 0.0) + offs).astype(BF16)
                a_last = a_s[r, r0 + CHUNK - 1:r0 + CHUNK, kls[h]]
                st_s[h] = st_s[h] * jnp.exp2(a_last) + upd[h]

        def intra_matmuls():
            for h in range(A_HEADS):
                intra[h] = jnp.dot(scores[h], v_s[r, rows, vls[h]], preferred_element_type=F32)

        def finish():
            for h in range(A_HEADS):
                oraw_s[rows, vls[h]] = o_inter[h] + intra[h]

        return [operands, small_matmuls, assemble, intra_matmuls, finish]

    def head_norm(j):
        cs = slice(2 * j * A_VDIM, 2 * (j + 1) * A_VDIM)
        o = oraw_s[:, cs]
        sq = o * o
        first = lax.broadcasted_iota(jnp.int32, (tm, 2 * A_VDIM), 1) < A_VDIM
        ms_a = jnp.sum(jnp.where(first, sq, 0.0), axis=-1, keepdims=True) * (1.0 / A_VDIM)
        ms_b = jnp.sum(jnp.where(first, 0.0, sq), axis=-1, keepdims=True) * (1.0 / A_VDIM)
        scale = jnp.where(first, lax.rsqrt(ms_a + EPS), lax.rsqrt(ms_b + EPS))
        oa_s[:, cs] = (o * scale * hg_ref[:, cs] * og_s[r, :, cs].astype(F32)).astype(BF16)

    def pool_group(gi):
        w = POOL_WINDOWS[gi]
        row_e = lax.broadcasted_iota(jnp.int32, (tm, B_GDIM), 0)
        pos = (tr * tm + row_e + pos0).astype(F32)
        if True:
            gl = slice(gi * B_GDIM, (gi + 1) * B_GDIM)
            cur = pin_s[r, :, gl]
            xs = jnp.concatenate([halo_s[:, gl], cur], axis=0)
            acc = xs
            span = 1
            while span < w:
                acc = acc + pltpu.roll(acc, span, 0)
                span *= 2
            cnt = jnp.clip(pos, 1.0, float(w))
            d = acc[POOL_HALO:] / cnt - cur
            y = jnp.dot(d.astype(BF16), wpool_ref[gi], preferred_element_type=F32)
            ob_s[:, gl] = (y * pscale_ref[:, gl]).astype(BF16)

    def pool_carry():
        halo_s[...] = pin_s[r, tm - POOL_HALO:tm, :]

    def merge_block(j):
        cs = slice(j * PROJ_BLK, (j + 1) * PROJ_BLK)
        ya = jnp.dot(oa_s[...], wa_ref[:, cs], preferred_element_type=F32)
        yb = jnp.dot(ob_s[...], wb_ref[:, cs], preferred_element_type=F32)
        mix_s[:, cs] = (ga_s[r, :, cs].astype(F32) * ya + gb_s[r, :, cs].astype(F32) * yb).astype(BF16)

    def out_block(j):
        cs = slice(j * PROJ_BLK, (j + 1) * PROJ_BLK)
        o_ref[0, :, cs] = jnp.dot(mix_s[...], wout_ref[:, cs], preferred_element_type=F32)

    P = functools.partial
    n_merge = D_MODEL // PROJ_BLK
    gap = None
    stages = [hgrn_stages(c) for c in range(n_chunks)]
    pools = [P(pool_group, gi) for gi in range(B_GROUPS)] + [pool_carry]
    seq_a = [stages[0][0]]
    for c in range(n_chunks):
        _, small_matmuls, assemble, intra_matmuls, finish = stages[c]
        seq_a += [small_matmuls, gap]
        if c + 1 < n_chunks:
            seq_a.append(stages[c + 1][0])
        seq_a += [assemble, intra_matmuls, gap, finish, gap]
        if pools:
            seq_a.append(pools.pop(0))
    seq_a += pools + [P(head_norm, j) for j in range(A_HEADS // 2)]
    heavy_a =[P(proj_block, k) for k in range(N_PROJ_BLK)]
    seq_b = [x for c in range(n_chunks) for x in (P(decay_chunk, c), gap)]
    heavy_b = [P(merge_block, j) for j in range(n_merge)] + [P(out_block, j) for j in range(n_merge)]
    norm_in()
    for seq, heavy in ((seq_a, heavy_a), (seq_b, heavy_b)):
        n_gaps = seq.count(gap)
        seen = 0
        for item in seq:
            if item is gap:
                for k in range(seen * len(heavy) // n_gaps, (seen + 1) * len(heavy) // n_gaps):
                    heavy[k]()
                seen += 1
            else:
                item()

    if emit_carry:
        st_out_ref[...] = st_s[...]
        pin_out_ref[...] = halo_s[...]


def _ffn_kernel(tm, emit_carry, x_ref, d_ref, halo0_ref, g2_ref, wup_ref, cw_ref, cb_ref, wdn_ref, gf_ref, *rest):
    if emit_carry:
        o_ref, halo_out_ref = rest[:2]
        rest = rest[2:]
    else:
        o_ref = rest[0]
        rest = rest[1:]
    z_s, u_s, halo_s, g_s, acc_s = rest
    t_idx = pl.program_id(1)

    @pl.when(t_idx == 0)
    def _():
        halo_s[...] = halo0_ref[...]

    z_s[...] = _rmsnorm(x_ref[0] + d_ref[0], g2_ref[...]).astype(BF16)

    def up(j):
        u = jnp.dot(z_s[...], wup_ref[:, j * FF_BLK:(j + 1) * FF_BLK], preferred_element_type=F32)
        v = jnp.dot(z_s[...], wup_ref[:, D_FF + j * FF_BLK:D_FF + (j + 1) * FF_BLK],
                    preferred_element_type=F32)
        return u, v

    for j in range(N_FF_BLK):
        cs = slice(j * FF_BLK, (j + 1) * FF_BLK)
        u, v = up(j)
        u_s[0:CONV_HALO, :] = halo_s[j]
        u_s[CONV_HALO:, :] = u
        halo_s[j] = u_s[tm:tm + CONV_HALO, :]
        uc = cb_ref[:, cs] + cw_ref[CONV_W - 1:CONV_W, cs] * u
        for k in range(CONV_W - 1):
            lag = CONV_W - 1 - k
            uc = uc + cw_ref[k:k + 1, cs] * u_s[CONV_HALO - lag:CONV_HALO - lag + tm, :]
        g_s[:, cs] = (jax.nn.silu(uc) * v).astype(BF16)
    for n in range(D_MODEL // FF_BLK):
        ns = slice(n * FF_BLK, (n + 1) * FF_BLK)
        acc_s[:, ns] = (x_ref[0, :, ns] + d_ref[0, :, ns]
                        + jnp.dot(g_s[...], wdn_ref[:, ns], preferred_element_type=F32))
    o_ref[0] = _rmsnorm(acc_s[...], gf_ref[...])
    if emit_carry:
        halo_out_ref[...] = halo_s[...]


def _mixer_call(x, st0, pin0, consts, tm, pos0, emit_carry):
    nb, t_len, d = x.shape
    n_t = t_len // tm
    n_tiles = nb * n_t

    def tile_map(shift):
        def index_map(g):
            tile = jnp.clip(g - shift, 0, n_tiles - 1)
            return (tile // n_t, tile % n_t, 0)
        return index_map

    def const_spec(shape):
        nd = len(shape)
        return pl.BlockSpec(shape, lambda g: (0,) * nd, pipeline_mode=pl.Buffered(1))

    ins = [x, st0, pin0] + consts
    out_shape = [jax.ShapeDtypeStruct((nb, t_len, d), F32)]
    out_specs = [pl.BlockSpec((1, tm, d), tile_map(1))]
    if emit_carry:
        out_shape += [jax.ShapeDtypeStruct(st0.shape, F32), jax.ShapeDtypeStruct(pin0.shape, F32)]
        out_specs += [pl.BlockSpec(st0.shape, lambda g: (0, 0, 0)), pl.BlockSpec(pin0.shape, lambda g: (0, 0))]
    slot = lambda cols, dtype: pltpu.VMEM((2, tm, cols), dtype)
    return pl.pallas_call(
        functools.partial(_mixer_kernel, tm, n_t, pos0, emit_carry),
        grid=(n_tiles + 1,),
        in_specs=[pl.BlockSpec((1, tm, d), tile_map(0))] + [const_spec(a.shape) for a in ins[1:]],
        out_specs=out_specs,
        out_shape=out_shape,
        scratch_shapes=[
            pltpu.VMEM((tm, D_MODEL), BF16),
            pltpu.VMEM((tm, D_MODEL), BF16),
            pltpu.VMEM((tm, A_WIDTH), BF16),
            pltpu.VMEM((tm, B_WIDTH), BF16),
            pltpu.VMEM((A_HEADS, A_VDIM, A_KDIM), F32),
            pltpu.VMEM((POOL_HALO, B_WIDTH), F32),
            pltpu.VMEM((2, A_HEADS, CHUNK, CHUNK), F32),
            pltpu.VMEM((tm, A_WIDTH), F32),
            slot(A_FDIM, F32),
            slot(A_FDIM, F32),
            slot(A_FDIM, F32),
            slot(A_FDIM, BF16),
            slot(A_FDIM, BF16),
            slot(A_WIDTH, BF16),
            slot(A_WIDTH, BF16),
            slot(B_WIDTH, F32),
            slot(D_MODEL, BF16),
            slot(D_MODEL, BF16),
        ],
        compiler_params=pltpu.CompilerParams(dimension_semantics=("arbitrary",),
                                             vmem_limit_bytes=VMEM_LIMIT_BYTES),
        name="mixer_meta" if emit_carry else "mixer",
    )(*ins)


def _ffn_call(x, delta, halo0, consts, tm, emit_carry):
    nb, t_len, d = x.shape
    tile_spec = pl.BlockSpec((1, tm, d), lambda b, t: (b, t, 0))

    def const_spec(shape):
        nd = len(shape)
        return pl.BlockSpec(shape, lambda b, t: (0,) * nd, pipeline_mode=pl.Buffered(1))

    ins = [x, delta, halo0] + consts
    out_shape = [jax.ShapeDtypeStruct((nb, t_len, d), F32)]
    out_specs = [tile_spec]
    if emit_carry:
        out_shape += [jax.ShapeDtypeStruct(halo0.shape, F32)]
        out_specs += [pl.BlockSpec(halo0.shape, lambda b, t: (0, 0, 0))]
    return pl.pallas_call(
        functools.partial(_ffn_kernel, tm, emit_carry),
        grid=(nb, t_len // tm),
        in_specs=[tile_spec, tile_spec] + [const_spec(a.shape) for a in ins[2:]],
        out_specs=out_specs,
        out_shape=out_shape,
        scratch_shapes=[
            pltpu.VMEM((tm, D_MODEL), BF16),
            pltpu.VMEM((CONV_HALO + tm, FF_BLK), F32),
            pltpu.VMEM((N_FF_BLK, CONV_HALO, FF_BLK), F32),
            pltpu.VMEM((tm, D_FF), BF16),
            pltpu.VMEM((tm, D_MODEL), F32),
        ],
        compiler_params=pltpu.CompilerParams(dimension_semantics=("arbitrary", "arbitrary"),
                                             vmem_limit_bytes=VMEM_LIMIT_BYTES),
        name="ffn_meta" if emit_carry else "ffn",
    )(*ins)


def kernel(x, meta_tokens, lb_logits, norm1_g, w_in, b_f, head_norm_g, w_pool, pool_scale, w_branch_a,
           w_branch_b, w_out, norm2_g, w_up, conv_w, conv_b, w_down, final_norm_g):
    B, S, D = x.shape
    assert D == D_MODEL and S % FFN_TM == 0 and meta_tokens.shape[0] == N_META

    row = lambda p: p.reshape(1, -1).astype(F32)
    mixer_consts = [
        lb_logits.astype(F32), row(norm1_g[0]), w_in[0].astype(BF16), row(b_f[0]), row(head_norm_g[0]),
        w_pool[0].astype(BF16), row(pool_scale[0]), w_branch_a[0].astype(BF16), w_branch_b[0].astype(BF16),
        w_out[0].astype(BF16),
    ]
    ffn_consts = [
        row(norm2_g[0]), w_up[0].astype(BF16), conv_w[0].astype(F32), row(conv_b[0]),
        w_down[0].astype(BF16), row(final_norm_g),
    ]

    xm = jnp.concatenate([jnp.zeros((PAD, D), x.dtype), meta_tokens.astype(x.dtype)], axis=0)[None]
    st0 = jnp.zeros((A_HEADS, A_VDIM, A_KDIM), F32)
    pin0 = jnp.zeros((POOL_HALO, B_WIDTH), F32)
    halo0 = jnp.zeros((N_FF_BLK, CONV_HALO, FF_BLK), F32)
    dm, st_m, pin_m = _mixer_call(xm, st0, pin0, mixer_consts, CHUNK, 1 - PAD, True)
    _, halo_m = _ffn_call(xm, dm, halo0, ffn_consts, CHUNK, True)

    (delta,) = _mixer_call(x, st_m, pin_m, mixer_consts, MAIN_TM, N_META + 1, False)
    (out,) = _ffn_call(x, delta, halo_m, ffn_consts, FFN_TM, False)
    return out
```
